```python
import math
import jax, jax.numpy as jnp
from jax import lax
import numpy as np

D_MODEL = 1024
BATCH = 2
SEQ = 16384
DEPTH = 1
DEC_BATCH = 32
DEC_SEQ = 64
PAST_LEN = 2048

CHUNK = 64
N_PREV_CHUNKS = 8
BAND_PAST = N_PREV_CHUNKS * CHUNK
BAND_LEN = BAND_PAST + CHUNK
N_HEADS = 8
HEAD_DIM = 64
ATTN_DIM = N_HEADS * HEAD_DIM
CONV_DIM = D_MODEL // 2
CONV_K = 31
MAX_REL = 128
FFN_DIM = ((8 * D_MODEL + 3 * 256 - 1) // (3 * 256)) * 256
N_IN = 2 * CONV_DIM + 3 * ATTN_DIM + 2 * D_MODEL
ATTN_SCALE = 1.0 / math.sqrt(HEAD_DIM)
NORM_EPS = 1e-6
NEG_INF = -1e30

kernel_name = "streaming_conformer_conv_band_attn_hybrid"


def rms_norm(x, g):
    xf = x.astype(jnp.float32)
    y = xf * lax.rsqrt(jnp.mean(xf * xf, axis=-1, keepdims=True) + NORM_EPS)
    return (y * g.astype(jnp.float32)).astype(x.dtype)


def layer_norm(x, g, b):
    xf = x.astype(jnp.float32)
    mu = jnp.mean(xf, axis=-1, keepdims=True)
    var = jnp.mean(jnp.square(xf - mu), axis=-1, keepdims=True)
    y = (xf - mu) * lax.rsqrt(var + NORM_EPS)
    return (y * g.astype(jnp.float32) + b.astype(jnp.float32)).astype(x.dtype)


def conv_module(u, hist, w_dw, b_dw, ln_g, ln_b, w_out):
    ext = jnp.concatenate([hist, u], axis=1)
    y = lax.conv_general_dilated(
        ext, w_dw[:, None, :], window_strides=(1,), padding='VALID',
        dimension_numbers=('NWC', 'WIO', 'NWC'), feature_group_count=CONV_DIM)
    y = jax.nn.silu(layer_norm(y + b_dw, ln_g, ln_b))
    return y @ w_out, ext[:, -(CONV_K - 1):]


def band_attend(q, k, v, q_pos, k_pos, rel_bias):
    s = jnp.einsum('bqhd,bkhd->bhqk', q, k).astype(jnp.float32) * ATTN_SCALE
    rel = jnp.clip(q_pos[:, None] - k_pos[None, :], -MAX_REL, MAX_REL) + MAX_REL
    s = s + rel_bias[:, rel].astype(jnp.float32)[None]
    qc = (q_pos // CHUNK)[:, None]
    kc = (k_pos // CHUNK)[None, :]
    mask = (k_pos[None, :] >= 0) & (kc <= qc) & (kc >= qc - N_PREV_CHUNKS)
    s = jnp.where(mask[None, None], s, NEG_INF)
    p = jax.nn.softmax(s, axis=-1).astype(v.dtype)
    return jnp.einsum('bhqk,bkhd->bqhd', p, v)


def prompt_band_attention(q, k, v, rel_bias):
    B, T, H, Dh = q.shape
    nc = T // CHUNK
    pad = ((0, 0), (BAND_PAST, 0), (0, 0), (0, 0))
    kp = jnp.pad(k, pad)
    vp = jnp.pad(v, pad)
    qc = q.reshape(B, nc, CHUNK, H, Dh).swapaxes(0, 1)

    def one_chunk(args):
        qn, n = args
        start = n * CHUNK
        kb = lax.dynamic_slice_in_dim(kp, start, BAND_LEN, axis=1)
        vb = lax.dynamic_slice_in_dim(vp, start, BAND_LEN, axis=1)
        q_pos = start + jnp.arange(CHUNK, dtype=jnp.int32)
        k_pos = start - BAND_PAST + jnp.arange(BAND_LEN, dtype=jnp.int32)
        return band_attend(qn, kb, vb, q_pos, k_pos, rel_bias)

    o = lax.map(one_chunk, (qc, jnp.arange(nc, dtype=jnp.int32)))
    return o.swapaxes(0, 1).reshape(B, T, H, Dh)


def encoder_layer(x, c, conv_hist, attend, p):
    B, T, _ = x.shape
    mod = jax.nn.silu(c) @ p['w_ada'] + p['b_ada']
    sh1, sc1, gt1, sh2, sc2, gt2 = [m[:, None, :] for m in jnp.split(mod, 6, axis=-1)]

    h = rms_norm(x, p['norm1_g']) * (1 + sc1) + sh1
    z = h @ p['w_in']
    offs = np.cumsum([CONV_DIM, CONV_DIM, ATTN_DIM, ATTN_DIM, ATTN_DIM, D_MODEL])
    glu_a, glu_b, q, k, v, g_conv, g_attn = jnp.split(z, offs, axis=-1)

    u = glu_a * jax.nn.sigmoid(glu_b)
    conv_out, conv_state = conv_module(u, conv_hist, p['w_dw'], p['b_dw'],
                                       p['conv_ln_g'], p['conv_ln_b'], p['w_conv_out'])

    q = rms_norm(q.reshape(B, T, N_HEADS, HEAD_DIM), p['q_norm_g'])
    k = rms_norm(k.reshape(B, T, N_HEADS, HEAD_DIM), p['k_norm_g'])
    v = v.reshape(B, T, N_HEADS, HEAD_DIM)
    o = attend(q, k, v, p['rel_bias']).reshape(B, T, ATTN_DIM)
    attn_out = o @ p['w_attn_out']

    merged = jax.nn.sigmoid(g_conv) * conv_out + jax.nn.sigmoid(g_attn) * attn_out
    x = x + gt1 * (merged @ p['w_o'])

    h2 = rms_norm(x, p['norm2_g']) * (1 + sc2) + sh2
    gate, up = jnp.split(h2 @ p['w_ffn_in'], 2, axis=-1)
    x = x + gt2 * ((jax.nn.silu(gate) * up) @ p['w_ffn_out'])
    return x, conv_state, k, v


def setup_inputs(seed: int = 0) -> dict:
    key = jax.random.key(seed)
    ks = iter(jax.random.split(key, 32))
    f32 = jnp.float32
    cache_len = min(BAND_PAST, PAST_LEN)

    def nrm(shape, scale):
        return jax.random.normal(next(ks), shape, f32) * scale

    return {
        "x_prompt": nrm((BATCH, SEQ, D_MODEL), 1.0),
        "x_sample": nrm((DEC_BATCH, DEC_SEQ, D_MODEL), 1.0),
        "c_prompt": nrm((BATCH, D_MODEL), 1.0),
        "c_sample": nrm((DEC_BATCH, D_MODEL), 1.0),
        "cache_conv": nrm((DEPTH, DEC_BATCH, CONV_K - 1, CONV_DIM), 1.0),
        "cache_k": nrm((DEPTH, DEC_BATCH, cache_len, N_HEADS, HEAD_DIM), 1.0),
        "cache_v": nrm((DEPTH, DEC_BATCH, cache_len, N_HEADS, HEAD_DIM), 1.0),
        "norm1_g": 1.0 + nrm((DEPTH, D_MODEL), 0.02),
        "norm2_g": 1.0 + nrm((DEPTH, D_MODEL), 0.02),
        "w_ada": nrm((DEPTH, D_MODEL, 6 * D_MODEL), 0.5 * D_MODEL ** -0.5),
        "b_ada": nrm((DEPTH, 6 * D_MODEL), 0.02),
        "w_in": nrm((DEPTH, D_MODEL, N_IN), D_MODEL ** -0.5),
        "w_dw": nrm((DEPTH, CONV_K, CONV_DIM), CONV_K ** -0.5),
        "b_dw": nrm((DEPTH, CONV_DIM), 0.02),
        "conv_ln_g": 1.0 + nrm((DEPTH, CONV_DIM), 0.02),
        "conv_ln_b": nrm((DEPTH, CONV_DIM), 0.02),
        "w_conv_out": nrm((DEPTH, CONV_DIM, D_MODEL), CONV_DIM ** -0.5),
        "q_norm_g": 1.0 + nrm((DEPTH, HEAD_DIM), 0.02),
        "k_norm_g": 1.0 + nrm((DEPTH, HEAD_DIM), 0.02),
        "rel_bias": nrm((DEPTH, N_HEADS, 2 * MAX_REL + 1), 0.5),
        "w_attn_out": nrm((DEPTH, ATTN_DIM, D_MODEL), ATTN_DIM ** -0.5),
        "w_o": nrm((DEPTH, D_MODEL, D_MODEL), D_MODEL ** -0.5),
        "w_ffn_in": nrm((DEPTH, D_MODEL, 2 * FFN_DIM), D_MODEL ** -0.5),
        "w_ffn_out": nrm((DEPTH, FFN_DIM, D_MODEL), FFN_DIM ** -0.5),
    }


def reference(x_prompt, x_sample, c_prompt, c_sample, cache_conv, cache_k, cache_v,
              norm1_g, norm2_g, w_ada, b_ada, w_in, w_dw, b_dw, conv_ln_g, conv_ln_b,
              w_conv_out, q_norm_g, k_norm_g, rel_bias, w_attn_out, w_o, w_ffn_in, w_ffn_out):
    t_new = x_sample.shape[1]
    prompt_state_len = min(BAND_PAST, x_prompt.shape[1])
    xp, xs = x_prompt, x_sample
    conv_p_l, kp_l, vp_l, conv_s_l, ks_l, vs_l = [], [], [], [], [], []

    for l in range(DEPTH):
        p = {
            'norm1_g': norm1_g[l], 'norm2_g': norm2_g[l], 'w_ada': w_ada[l], 'b_ada': b_ada[l],
            'w_in': w_in[l], 'w_dw': w_dw[l], 'b_dw': b_dw[l], 'conv_ln_g': conv_ln_g[l],
            'conv_ln_b': conv_ln_b[l], 'w_conv_out': w_conv_out[l], 'q_norm_g': q_norm_g[l],
            'k_norm_g': k_norm_g[l], 'rel_bias': rel_bias[l], 'w_attn_out': w_attn_out[l],
            'w_o': w_o[l], 'w_ffn_in': w_ffn_in[l], 'w_ffn_out': w_ffn_out[l],
        }

        hist0 = jnp.zeros((xp.shape[0], CONV_K - 1, CONV_DIM), xp.dtype)
        xp, conv_p, k_p, v_p = encoder_layer(xp, c_prompt, hist0, prompt_band_attention, p)
        conv_p_l.append(conv_p)
        kp_l.append(k_p[:, -prompt_state_len:])
        vp_l.append(v_p[:, -prompt_state_len:])

        ck, cv = cache_k[l], cache_v[l]
        cache_len = ck.shape[1]

        def sample_attend(q, k, v, rb, ck=ck, cv=cv, cache_len=cache_len):
            k_all = jnp.concatenate([ck, k], axis=1)
            v_all = jnp.concatenate([cv, v], axis=1)
            q_pos = PAST_LEN + jnp.arange(t_new, dtype=jnp.int32)
            k_pos = jnp.concatenate(
                [PAST_LEN - cache_len + jnp.arange(cache_len, dtype=jnp.int32), q_pos])
            return band_attend(q, k_all, v_all, q_pos, k_pos, rb)

        xs, conv_s, k_s, v_s = encoder_layer(xs, c_sample, cache_conv[l], sample_attend, p)
        conv_s_l.append(conv_s)
        ks_l.append(k_s)
        vs_l.append(v_s)

    conv_state_prompt = jnp.stack(conv_p_l)
    k_state_prompt = jnp.stack(kp_l)
    v_state_prompt = jnp.stack(vp_l)
    conv_state_sample = jnp.stack(conv_s_l)
    k_new_sample = jnp.stack(ks_l)
    v_new_sample = jnp.stack(vs_l)
    return (xp, xs, conv_state_prompt, k_state_prompt, v_state_prompt,
            conv_state_sample, k_new_sample, v_new_sample)
```

```python
import functools
import math

import jax
import jax.numpy as jnp
from jax import lax
from jax.experimental import pallas as pl
from jax.experimental.pallas import tpu as pltpu

F32 = jnp.float32
BF16 = jnp.bfloat16

CHUNK = 64
N_PREV_CHUNKS = 8
BAND_PAST = N_PREV_CHUNKS * CHUNK
HEAD_DIM = 64
CONV_K = 31
MAX_REL = 128
NORM_EPS = 1e-6
NEG_INF = -1e30
LANES = 128
SUBLANES = 8
CONV_ROW_STRIDE = 4
HIST_ROWS = 32
ROLL_W = 1024
VMEM_LIMIT = 56 * 1024 * 1024


def _cparams(n_axes):
    return pltpu.CompilerParams(
        dimension_semantics=("arbitrary",) * n_axes, vmem_limit_bytes=VMEM_LIMIT)


def _resident(shape):
    nd = len(shape)
    return pl.BlockSpec(shape, lambda *_: (0,) * nd, pipeline_mode=pl.Buffered(1))


def _sigmoid(x):
    return 1.0 / (1.0 + jnp.exp(-x))


def _silu(x):
    return x * _sigmoid(x)


def _ada_kernel(c_ref, w_ref, b_ref, o_ref):
    a = _silu(c_ref[...]).astype(BF16)
    o_ref[...] = jnp.dot(a, w_ref[...].astype(BF16), preferred_element_type=F32) + b_ref[...]


def _ada(c_all, w_ada, b_ada):
    n, d = c_all.shape
    n_out = w_ada.shape[1]
    tn = 1024
    return pl.pallas_call(
        _ada_kernel,
        grid=(n_out // tn,),
        in_specs=[pl.BlockSpec((n, d), lambda j: (0, 0)),
                  pl.BlockSpec((d, tn), lambda j: (0, j)),
                  pl.BlockSpec((1, tn), lambda j: (0, j))],
        out_specs=pl.BlockSpec((n, tn), lambda j: (0, j)),
        out_shape=jax.ShapeDtypeStruct((n, n_out), F32),
        compiler_params=_cparams(1),
        name="ada",
    )(c_all, w_ada, b_ada.reshape(1, n_out))


def _inproj_kernel(x_ref, mod_ref, g1_ref, w_ref, qg_ref, kg_ref, gm_ref,
                   u_ref, q_ref, kf_ref, vf_ref, kb_ref, vb_ref, gc_ref, ga_ref,
                   *, S, R, D, CD, AD):
    rows = S * R
    x = x_ref[...]
    ms = jnp.mean(x * x, axis=-1, keepdims=True)
    xn = x * lax.rsqrt(ms + NORM_EPS) * g1_ref[...]
    sh = mod_ref[:, :, 0:D]
    sc = mod_ref[:, :, D:2 * D]
    h = (xn * (1.0 + sc) + sh).reshape(rows, D).astype(BF16)

    def sec(lo, width):
        return jnp.dot(h, w_ref[:, lo:lo + width], preferred_element_type=F32)

    def head_rms(z, g_ref):
        msq = jnp.dot((z * z).astype(BF16), gm_ref[...], preferred_element_type=F32)
        return z * lax.rsqrt(msq + NORM_EPS) * g_ref[...]

    u = sec(0, CD) * _sigmoid(sec(CD, CD))
    u_ref[...] = u.reshape(S, R, CD)

    o = 2 * CD
    qn = head_rms(sec(o, AD), qg_ref) * (1.0 / math.sqrt(HEAD_DIM))
    q_ref[...] = qn.astype(BF16).reshape(S, R, AD)
    kn = head_rms(sec(o + AD, AD), kg_ref)
    kf_ref[...] = kn.reshape(S, R, AD)
    kb_ref[...] = kn.astype(BF16).reshape(S, R, AD)
    v = sec(o + 2 * AD, AD)
    vf_ref[...] = v.reshape(S, R, AD)
    vb_ref[...] = v.astype(BF16).reshape(S, R, AD)
    o = o + 3 * AD
    gc_ref[...] = _sigmoid(sec(o, D)).astype(BF16).reshape(S, R, D)
    ga_ref[...] = _sigmoid(sec(o + D, D)).astype(BF16).reshape(S, R, D)


def _in_proj(x, mod, g1, w_in, qg, kg, gm, *, S, R):
    nseq, T, D = x.shape
    AD = qg.shape[1]
    CD = (w_in.shape[1] - 3 * AD - 2 * D) // 2
    grid = (nseq // S, T // R)
    tile = lambda c: pl.BlockSpec((S, R, c), lambda i, t: (i, t, 0))
    outs = [(CD, F32), (AD, BF16), (AD, F32), (AD, F32), (AD, BF16), (AD, BF16), (D, BF16), (D, BF16)]
    return pl.pallas_call(
        functools.partial(_inproj_kernel, S=S, R=R, D=D, CD=CD, AD=AD),
        grid=grid,
        in_specs=[tile(D),
                  pl.BlockSpec((S, 1, mod.shape[2]), lambda i, t: (i, 0, 0)),
                  _resident(g1.shape), _resident(w_in.shape), _resident(qg.shape),
                  _resident(kg.shape), _resident(gm.shape)],
        out_specs=[tile(c) for c, _ in outs],
        out_shape=[jax.ShapeDtypeStruct((nseq, T, c), dt) for c, dt in outs],
        compiler_params=_cparams(2),
        name="in_proj",
    )(x, mod, g1, w_in, qg, kg, gm)


def _conv_kernel(u_ref, hist_ref, w_ref, b_ref, lg_ref, lb_ref, y_ref, ext, ybuf,
                 *, S, R, NS):
    t = pl.program_id(1)
    lead = HIST_ROWS - (CONV_K - 1)
    ST = CONV_ROW_STRIDE
    RB = SUBLANES * ST
    nb = R // RB

    @pl.when(t == 0)
    def _():
        for s in range(S):
            for c in range(NS):
                ext[s * NS + c, lead:HIST_ROWS, :] = hist_ref[s, :, c * LANES:(c + 1) * LANES]

    @pl.when(t > 0)
    def _():
        for s in range(S):
            for c in range(NS):
                ext[s * NS + c, 0:HIST_ROWS, :] = ext[s * NS + c, R:R + HIST_ROWS, :]

    for s in range(S):
        for c in range(NS):
            ext[s * NS + c, HIST_ROWS:HIST_ROWS + R, :] = u_ref[s, :, c * LANES:(c + 1) * LANES]

    def row_block(i, carry):
        s = i >> (nb.bit_length() - 1)
        r0 = (i & (nb - 1)) * RB
        for c in range(NS):
            accs = [jnp.zeros((SUBLANES, LANES), F32) for _ in range(ST)]
            for j in range(CONV_K):
                wj = w_ref[c, pl.ds(j, SUBLANES, stride=0), :]
                for q in range(ST):
                    rows = ext[s * NS + c, pl.ds(r0 + lead + q + j, SUBLANES, stride=ST), :]
                    accs[q] = accs[q] + rows * wj
            for q in range(ST):
                ybuf[s * NS + c, pl.ds(r0 + q, SUBLANES, stride=ST), :] = accs[q]
        return carry

    lax.fori_loop(0, S * nb, row_block, 0)

    for s in range(S):
        y = jnp.concatenate([ybuf[s * NS + c] for c in range(NS)], axis=-1) + b_ref[...]
        mu = jnp.mean(y, axis=-1, keepdims=True)
        yc = y - mu
        var = jnp.mean(yc * yc, axis=-1, keepdims=True)
        yn = yc * lax.rsqrt(var + NORM_EPS) * lg_ref[...] + lb_ref[...]
        y_ref[s] = _silu(yn).astype(BF16)


def _conv(u, hist, w_dw, b_dw, ln_g, ln_b, *, S, R):
    nseq, T, CD = u.shape
    NS = CD // LANES
    RB = SUBLANES * CONV_ROW_STRIDE
    assert R % RB == 0 and (R // RB) & (R // RB - 1) == 0
    w_slabs = jnp.pad(w_dw, ((0, -CONV_K % SUBLANES), (0, 0))).reshape(-1, NS, LANES).swapaxes(0, 1)
    return pl.pallas_call(
        functools.partial(_conv_kernel, S=S, R=R, NS=NS),
        grid=(nseq // S, T // R),
        in_specs=[pl.BlockSpec((S, R, CD), lambda i, t: (i, t, 0)),
                  pl.BlockSpec((S, CONV_K - 1, CD), lambda i, t: (i, 0, 0)),
                  _resident(w_slabs.shape), _resident(b_dw.shape),
                  _resident(ln_g.shape), _resident(ln_b.shape)],
        out_specs=pl.BlockSpec((S, R, CD), lambda i, t: (i, t, 0)),
        out_shape=jax.ShapeDtypeStruct((nseq, T, CD), BF16),
        scratch_shapes=[pltpu.VMEM((S * NS, R + HIST_ROWS, LANES), F32),
                        pltpu.VMEM((S * NS, R, LANES), F32)],
        compiler_params=_cparams(2),
        name="conv",
    )(u, hist, w_slabs, b_dw, ln_g, ln_b)


def _bias_kernel(e_ref, o_ref, *, TQ, TK, first_start):
    start = first_start + pl.program_id(0) * TQ
    row = jnp.broadcast_to(e_ref[0], (TQ, ROLL_W))
    b = pltpu.roll(row, 0, 1, stride=1, stride_axis=0)[:, :TK]
    qi = lax.broadcasted_iota(jnp.int32, (TQ, TK), 0)
    ki = lax.broadcasted_iota(jnp.int32, (TQ, TK), 1)
    qc = qi // CHUNK
    kc = ki // CHUNK
    valid = (kc >= qc) & (kc <= qc + N_PREV_CHUNKS) & (ki + start >= BAND_PAST)
    o_ref[0, 0] = jnp.where(valid, b, NEG_INF)


def _bias_tiles(rel_bias, *, TQ, n_var, first_start):
    H = rel_bias.shape[0]
    TK = BAND_PAST + TQ
    assert TQ + TK <= ROLL_W
    far = rel_bias[:, 2 * MAX_REL:]
    near = rel_bias[:, :1]
    by_m = jnp.concatenate([
        jnp.broadcast_to(far, (H, BAND_PAST - MAX_REL)),
        rel_bias[:, ::-1],
        jnp.broadcast_to(near, (H, ROLL_W - (BAND_PAST + MAX_REL) - 1)),
    ], axis=1)
    e = jnp.concatenate([by_m[:, :TK], jnp.broadcast_to(far, (H, ROLL_W - TK))], axis=1)
    e = e.reshape(H, 1, ROLL_W)
    return pl.pallas_call(
        functools.partial(_bias_kernel, TQ=TQ, TK=TK, first_start=first_start),
        grid=(n_var, H),
        in_specs=[pl.BlockSpec((1, 1, ROLL_W), lambda j, h: (h, 0, 0))],
        out_specs=pl.BlockSpec((1, 1, TQ, TK), lambda j, h: (j, h, 0, 0)),
        out_shape=jax.ShapeDtypeStruct((n_var, H, TQ, TK), F32),
        compiler_params=_cparams(2),
        name="bias_tiles",
    )(e)


def _attn_kernel(q_ref, k_ref, v_ref, hk_ref, hv_ref, bias_ref, o_ref, kbuf, vbuf,
                 *, S, TQ, H):
    t = pl.program_id(1)

    @pl.when(t == 0)
    def _():
        for s in range(S):
            kbuf[s, 0:BAND_PAST, :] = hk_ref[s]
            vbuf[s, 0:BAND_PAST, :] = hv_ref[s]

    @pl.when(t > 0)
    def _():
        for s in range(S):
            kbuf[s, 0:BAND_PAST, :] = kbuf[s, TQ:TQ + BAND_PAST, :]
            vbuf[s, 0:BAND_PAST, :] = vbuf[s, TQ:TQ + BAND_PAST, :]

    for s in range(S):
        kbuf[s, BAND_PAST:BAND_PAST + TQ, :] = k_ref[s]
        vbuf[s, BAND_PAST:BAND_PAST + TQ, :] = v_ref[s]

    low = lax.broadcasted_iota(jnp.int32, (1, LANES), 1) < HEAD_DIM
    for s in range(S):
        for p in range(H // 2):
            cols = slice(p * LANES, (p + 1) * LANES)
            q2 = q_ref[s, :, cols]
            k2 = kbuf[s, :, cols]
            v2 = vbuf[s, :, cols]
            outs = []
            for half in range(2):
                sel = low if half == 0 else jnp.logical_not(low)
                qh = jnp.where(sel, q2, jnp.zeros_like(q2))
                sc = lax.dot_general(qh, k2, (((1,), (1,)), ((), ())),
                                     preferred_element_type=F32)
                sc = sc + bias_ref[0, 2 * p + half]
                mx = jnp.max(sc, axis=-1, keepdims=True)
                pr = jnp.exp(sc - mx)
                den = jnp.sum(pr, axis=-1, keepdims=True)
                oh = jnp.dot(pr.astype(BF16), v2, preferred_element_type=F32)
                outs.append(oh / den)
            o_ref[s, :, cols] = jnp.where(low, outs[0], outs[1]).astype(BF16)


def _attention(q, kb, vb, hk, hv, bias, *, S, TQ):
    nseq, T, AD = q.shape
    H = AD // HEAD_DIM
    TK = BAND_PAST + TQ
    last_var = bias.shape[0] - 1
    tile = pl.BlockSpec((S, TQ, AD), lambda i, t: (i, t, 0))
    hist = pl.BlockSpec((S, BAND_PAST, AD), lambda i, t: (i, 0, 0))
    return pl.pallas_call(
        functools.partial(_attn_kernel, S=S, TQ=TQ, H=H),
        grid=(nseq // S, T // TQ),
        in_specs=[tile, tile, tile, hist, hist,
                  pl.BlockSpec((1, H, TQ, TK), lambda i, t: (jnp.minimum(t, last_var), 0, 0, 0))],
        out_specs=tile,
        out_shape=jax.ShapeDtypeStruct((nseq, T, AD), BF16),
        scratch_shapes=[pltpu.VMEM((S, TK, AD), BF16), pltpu.VMEM((S, TK, AD), BF16)],
        compiler_params=_cparams(2),
        name="attn",
    )(q, kb, vb, hk, hv, bias)


def _outffn_kernel(x_ref, mod_ref, cy_ref, ao_ref, gc_ref, ga_ref, g2_ref,
                   wc_ref, wa_ref, wo_ref, wfi_ref, wfo_ref, y_ref, act,
                   *, S, R, D, F, FC):
    rows = S * R
    conv_out = jnp.dot(cy_ref[...].reshape(rows, -1), wc_ref[...], preferred_element_type=F32)
    attn_out = jnp.dot(ao_ref[...].reshape(rows, -1), wa_ref[...], preferred_element_type=F32)
    merged = (gc_ref[...].reshape(rows, D).astype(F32) * conv_out
              + ga_ref[...].reshape(rows, D).astype(F32) * attn_out)
    upd = jnp.dot(merged.astype(BF16), wo_ref[...], preferred_element_type=F32)
    gt1 = mod_ref[:, :, 2 * D:3 * D]
    x1 = x_ref[...] + gt1 * upd.reshape(S, R, D)

    ms = jnp.mean(x1 * x1, axis=-1, keepdims=True)
    xn = x1 * lax.rsqrt(ms + NORM_EPS) * g2_ref[...]
    sh2 = mod_ref[:, :, 3 * D:4 * D]
    sc2 = mod_ref[:, :, 4 * D:5 * D]
    h2 = (xn * (1.0 + sc2) + sh2).reshape(rows, D).astype(BF16)

    for c in range(F // FC):
        gate = jnp.dot(h2, wfi_ref[:, c * FC:(c + 1) * FC], preferred_element_type=F32)
        up = jnp.dot(h2, wfi_ref[:, F + c * FC:F + (c + 1) * FC], preferred_element_type=F32)
        act[:, c * FC:(c + 1) * FC] = (_silu(gate) * up).astype(BF16)
    ffn = jnp.dot(act[...], wfo_ref[...], preferred_element_type=F32)
    gt2 = mod_ref[:, :, 5 * D:6 * D]
    y_ref[...] = x1 + gt2 * ffn.reshape(S, R, D)


def _out_ffn(x, mod, cy, ao, gc, ga, g2, wc, wa, wo, wfi, wfo, *, S, R):
    nseq, T, D = x.shape
    F = wfo.shape[0]
    FC = 256
    tile = lambda c: pl.BlockSpec((S, R, c), lambda i, t: (i, t, 0))
    return pl.pallas_call(
        functools.partial(_outffn_kernel, S=S, R=R, D=D, F=F, FC=FC),
        grid=(nseq // S, T // R),
        in_specs=[tile(D),
                  pl.BlockSpec((S, 1, mod.shape[2]), lambda i, t: (i, 0, 0)),
                  tile(cy.shape[2]), tile(ao.shape[2]), tile(D), tile(D),
                  _resident(g2.shape), _resident(wc.shape), _resident(wa.shape),
                  _resident(wo.shape), _resident(wfi.shape), _resident(wfo.shape)],
        out_specs=tile(D),
        out_shape=jax.ShapeDtypeStruct((nseq, T, D), F32),
        scratch_shapes=[pltpu.VMEM((S * R, F), BF16)],
        compiler_params=_cparams(2),
        name="out_ffn",
    )(x, mod, cy, ao, gc, ga, g2, wc, wa, wo, wfi, wfo)


def _tiling(nseq, T, rows):
    R = min(T, rows)
    S = max(1, min(nseq, rows // R))
    assert T % R == 0 and nseq % S == 0
    return S, R


def _layer(x, mod, conv_hist, hk, hv, bias, p, *, TQ):
    nseq, T, D = x.shape
    S, R = _tiling(nseq, T, 512)
    u, q, kf, vf, kb, vb, gc, ga = _in_proj(
        x, mod, p["g1"], p["w_in"], p["qg"], p["kg"], p["gm"], S=S, R=R)
    cy = _conv(u, conv_hist, p["w_dw"], p["b_dw"], p["ln_g"], p["ln_b"], S=S, R=R)
    Sa, _ = _tiling(nseq, T, TQ * max(1, 512 // T))
    ao = _attention(q, kb, vb, hk, hv, bias, S=Sa, TQ=TQ)
    y = _out_ffn(x, mod, cy, ao, gc, ga, p["g2"], p["w_conv_out"], p["w_attn_out"],
                 p["w_o"], p["w_ffn_in"], p["w_ffn_out"], S=S, R=R)
    return y, u, kf, vf


def kernel(x_prompt, x_sample, c_prompt, c_sample, cache_conv, cache_k, cache_v, norm1_g, norm2_g, w_ada, b_ada, w_in, w_dw, b_dw, conv_ln_g, conv_ln_b, w_conv_out, q_norm_g, k_norm_g, rel_bias, w_attn_out, w_o, w_ffn_in, w_ffn_out):
    depth = norm1_g.shape[0]
    B, T, D = x_prompt.shape
    BS, TS, _ = x_sample.shape
    H, Dh = cache_k.shape[3], cache_k.shape[4]
    AD = H * Dh
    CD = w_dw.shape[2]
    cache_len = cache_k.shape[2]
    assert cache_len == BAND_PAST and TS == CHUNK and T % CHUNK == 0 and Dh == HEAD_DIM
    state_len = min(BAND_PAST, T)
    TQP = 128
    n_c = B + BS
    n_pad = -n_c % 8

    hid = jnp.arange(AD) // Dh
    gm = jnp.where(hid[:, None] == hid[None, :], 1.0 / Dh, 0.0).astype(BF16)

    xp, xs = x_prompt, x_sample
    outs = [[] for _ in range(6)]
    for l in range(depth):
        p = {
            "g1": norm1_g[l].reshape(1, D), "g2": norm2_g[l].reshape(1, D),
            "w_in": w_in[l].astype(BF16),
            "qg": jnp.tile(q_norm_g[l], H).reshape(1, AD), "kg": jnp.tile(k_norm_g[l], H).reshape(1, AD),
            "gm": gm,
            "w_dw": w_dw[l], "b_dw": b_dw[l].reshape(1, CD),
            "ln_g": conv_ln_g[l].reshape(1, CD), "ln_b": conv_ln_b[l].reshape(1, CD),
            "w_conv_out": w_conv_out[l].astype(BF16), "w_attn_out": w_attn_out[l].astype(BF16),
            "w_o": w_o[l].astype(BF16), "w_ffn_in": w_ffn_in[l].astype(BF16),
            "w_ffn_out": w_ffn_out[l].astype(BF16),
        }
        c_all = jnp.pad(jnp.concatenate([c_prompt, c_sample], axis=0), ((0, n_pad), (0, 0)))
        mod = _ada(c_all, w_ada[l], b_ada[l])
        mod_p = mod[:B].reshape(B, 1, -1)
        mod_s = mod[B:n_c].reshape(BS, 1, -1)

        bias_p = _bias_tiles(rel_bias[l], TQ=TQP, n_var=BAND_PAST // TQP + 1, first_start=0)
        bias_s = _bias_tiles(rel_bias[l], TQ=TS, n_var=1, first_start=BAND_PAST)

        zeros_c = jnp.zeros((B, CONV_K - 1, CD), F32)
        zeros_kv = jnp.zeros((B, BAND_PAST, AD), BF16)
        xp, u_p, kf_p, vf_p = _layer(xp, mod_p, zeros_c, zeros_kv, zeros_kv, bias_p, p, TQ=TQP)
        hk = cache_k[l].reshape(BS, cache_len, AD).astype(BF16)
        hv = cache_v[l].reshape(BS, cache_len, AD).astype(BF16)
        xs, u_s, kf_s, vf_s = _layer(xs, mod_s, cache_conv[l], hk, hv, bias_s, p, TQ=TS)

        outs[0].append(u_p[:, T - (CONV_K - 1):])
        outs[1].append(kf_p[:, T - state_len:].reshape(B, state_len, H, Dh))
        outs[2].append(vf_p[:, T - state_len:].reshape(B, state_len, H, Dh))
        outs[3].append(u_s[:, TS - (CONV_K - 1):])
        outs[4].append(kf_s.reshape(BS, TS, H, Dh))
        outs[5].append(vf_s.reshape(BS, TS, H, Dh))

    return (xp, xs) + tuple(jnp.stack(o) for o in outs)
```

```python
import functools
import math

import jax
import jax.numpy as jnp
from jax import lax
from jax.experimental import pallas as pl
from jax.experimental.pallas import tpu as pltpu

F32 = jnp.float32
BF16 = jnp.bfloat16

CHUNK = 64
N_PREV_CHUNKS = 8
BAND_PAST = N_PREV_CHUNKS * CHUNK
HEAD_DIM = 64
CONV_K = 31
MAX_REL = 128
NORM_EPS = 1e-6
NEG_INF = -1e30
LANES = 128
SUBLANES = 8
CONV_ROW_STRIDE = 4
HIST_ROWS = 32
ROLL_W = 1024
VMEM_LIMIT = 56 * 1024 * 1024


def _cparams(n_axes):
    return pltpu.CompilerParams(
        dimension_semantics=("arbitrary",) * n_axes, vmem_limit_bytes=VMEM_LIMIT)


def _resident(shape):
    nd = len(shape)
    return pl.BlockSpec(shape, lambda *_: (0,) * nd, pipeline_mode=pl.Buffered(1))


def _sigmoid(x):
    return 1.0 / (1.0 + jnp.exp(-x))


def _silu(x):
    return x * _sigmoid(x)


def _ada_kernel(c_ref, w_ref, b_ref, o_ref):
    a = _silu(c_ref[...]).astype(BF16)
    o_ref[...] = jnp.dot(a, w_ref[...].astype(BF16), preferred_element_type=F32) + b_ref[...]


def _ada(c_all, w_ada, b_ada):
    n, d = c_all.shape
    n_out = w_ada.shape[1]
    tn = 1024
    return pl.pallas_call(
        _ada_kernel,
        grid=(n_out // tn,),
        in_specs=[pl.BlockSpec((n, d), lambda j: (0, 0)),
                  pl.BlockSpec((d, tn), lambda j: (0, j)),
                  pl.BlockSpec((1, tn), lambda j: (0, j))],
        out_specs=pl.BlockSpec((n, tn), lambda j: (0, j)),
        out_shape=jax.ShapeDtypeStruct((n, n_out), F32),
        compiler_params=_cparams(1),
        name="ada",
    )(c_all, w_ada, b_ada.reshape(1, n_out))


def _inproj_kernel(x_ref, mod_ref, g1_ref, w_ref, qg_ref, kg_ref, gm_ref,
                   u_ref, q_ref, kf_ref, vf_ref, kb_ref, vb_ref, gc_ref, ga_ref,
                   *, S, R, D, CD, AD):
    rows = S * R
    x = x_ref[...]
    ms = jnp.mean(x * x, axis=-1, keepdims=True)
    xn = x * lax.rsqrt(ms + NORM_EPS) * g1_ref[...]
    sh = mod_ref[:, :, 0:D]
    sc = mod_ref[:, :, D:2 * D]
    h = (xn * (1.0 + sc) + sh).reshape(rows, D).astype(BF16)

    def sec(lo, width):
        return jnp.dot(h, w_ref[:, lo:lo + width], preferred_element_type=F32)

    def head_rms(z, g_ref):
        msq = jnp.dot((z * z).astype(BF16), gm_ref[...], preferred_element_type=F32)
        return z * lax.rsqrt(msq + NORM_EPS) * g_ref[...]

    u = sec(0, CD) * _sigmoid(sec(CD, CD))
    u_ref[...] = u.reshape(S, R, CD)

    o = 2 * CD
    qn = head_rms(sec(o, AD), qg_ref) * (1.0 / math.sqrt(HEAD_DIM))
    q_ref[...] = qn.astype(BF16).reshape(S, R, AD)
    kn = head_rms(sec(o + AD, AD), kg_ref)
    kf_ref[...] = kn.reshape(S, R, AD)
    kb_ref[...] = kn.astype(BF16).reshape(S, R, AD)
    v = sec(o + 2 * AD, AD)
    vf_ref[...] = v.reshape(S, R, AD)
    vb_ref[...] = v.astype(BF16).reshape(S, R, AD)
    o = o + 3 * AD
    gc_ref[...] = _sigmoid(sec(o, D)).astype(BF16).reshape(S, R, D)
    ga_ref[...] = _sigmoid(sec(o + D, D)).astype(BF16).reshape(S, R, D)


def _in_proj(x, mod, g1, w_in, qg, kg, gm, *, S, R):
    nseq, T, D = x.shape
    AD = qg.shape[1]
    CD = (w_in.shape[1] - 3 * AD - 2 * D) // 2
    grid = (nseq // S, T // R)
    tile = lambda c: pl.BlockSpec((S, R, c), lambda i, t: (i, t, 0))
    outs = [(CD, F32), (AD, BF16), (AD, F32), (AD, F32), (AD, BF16), (AD, BF16), (D, BF16), (D, BF16)]
    return pl.pallas_call(
        functools.partial(_inproj_kernel, S=S, R=R, D=D, CD=CD, AD=AD),
        grid=grid,
        in_specs=[tile(D),
                  pl.BlockSpec((S, 1, mod.shape[2]), lambda i, t: (i, 0, 0)),
                  _resident(g1.shape), _resident(w_in.shape), _resident(qg.shape),
                  _resident(kg.shape), _resident(gm.shape)],
        out_specs=[tile(c) for c, _ in outs],
        out_shape=[jax.ShapeDtypeStruct((nseq, T, c), dt) for c, dt in outs],
        compiler_params=_cparams(2),
        name="in_proj",
    )(x, mod, g1, w_in, qg, kg, gm)


def _conv_kernel(u_ref, hist_ref, w_ref, b_ref, lg_ref, lb_ref, y_ref, ext, ybuf,
                 *, S, R, NS):
    t = pl.program_id(1)
    lead = HIST_ROWS - (CONV_K - 1)
    ST = CONV_ROW_STRIDE
    RB = SUBLANES * ST
    nb = R // RB

    @pl.when(t == 0)
    def _():
        for s in range(S):
            for c in range(NS):
                ext[s * NS + c, lead:HIST_ROWS, :] = hist_ref[s, :, c * LANES:(c + 1) * LANES]

    @pl.when(t > 0)
    def _():
        for s in range(S):
            for c in range(NS):
                ext[s * NS + c, 0:HIST_ROWS, :] = ext[s * NS + c, R:R + HIST_ROWS, :]

    for s in range(S):
        for c in range(NS):
            ext[s * NS + c, HIST_ROWS:HIST_ROWS + R, :] = u_ref[s, :, c * LANES:(c + 1) * LANES]

    def row_block(i, carry):
        s = i >> (nb.bit_length() - 1)
        r0 = (i & (nb - 1)) * RB
        for c in range(NS):
            accs = [jnp.zeros((SUBLANES, LANES), F32) for _ in range(ST)]
            for j in range(CONV_K):
                wj = jnp.broadcast_to(w_ref[c, j:j + 1, :], (SUBLANES, LANES))
                for q in range(ST):
                    rows = ext[s * NS + c, pl.ds(r0 + lead + q + j, SUBLANES, stride=ST), :]
                    accs[q] = accs[q] + rows * wj
            for q in range(ST):
                ybuf[s * NS + c, pl.ds(r0 + q, SUBLANES, stride=ST), :] = accs[q]
        return carry

    lax.fori_loop(0, S * nb, row_block, 0)

    for s in range(S):
        y = jnp.concatenate([ybuf[s * NS + c] for c in range(NS)], axis=-1) + b_ref[...]
        mu = jnp.mean(y, axis=-1, keepdims=True)
        yc = y - mu
        var = jnp.mean(yc * yc, axis=-1, keepdims=True)
        yn = yc * lax.rsqrt(var + NORM_EPS) * lg_ref[...] + lb_ref[...]
        y_ref[s] = _silu(yn).astype(BF16)


def _conv(u, hist, w_dw, b_dw, ln_g, ln_b, *, S, R):
    nseq, T, CD = u.shape
    NS = CD // LANES
    RB = SUBLANES * CONV_ROW_STRIDE
    assert R % RB == 0 and (R // RB) & (R // RB - 1) == 0
    w_slabs = jnp.pad(w_dw, ((0, -CONV_K % SUBLANES), (0, 0))).reshape(-1, NS, LANES).swapaxes(0, 1)
    return pl.pallas_call(
        functools.partial(_conv_kernel, S=S, R=R, NS=NS),
        grid=(nseq // S, T // R),
        in_specs=[pl.BlockSpec((S, R, CD), lambda i, t: (i, t, 0)),
                  pl.BlockSpec((S, CONV_K - 1, CD), lambda i, t: (i, 0, 0)),
                  _resident(w_slabs.shape), _resident(b_dw.shape),
                  _resident(ln_g.shape), _resident(ln_b.shape)],
        out_specs=pl.BlockSpec((S, R, CD), lambda i, t: (i, t, 0)),
        out_shape=jax.ShapeDtypeStruct((nseq, T, CD), BF16),
        scratch_shapes=[pltpu.VMEM((S * NS, R + HIST_ROWS, LANES), F32),
                        pltpu.VMEM((S * NS, R, LANES), F32)],
        compiler_params=_cparams(2),
        name="conv",
    )(u, hist, w_slabs, b_dw, ln_g, ln_b)


def _bias_kernel(e_ref, o_ref, *, TQ, TK, NB, first_start):
    start = first_start + pl.program_id(0) * TQ
    row = jnp.broadcast_to(e_ref[0], (TQ, ROLL_W))
    b = pltpu.roll(row, 0, 1, stride=1, stride_axis=0)[:, :TK]
    qi = lax.broadcasted_iota(jnp.int32, (TQ, TK), 0)
    ki = lax.broadcasted_iota(jnp.int32, (TQ, TK), 1)
    qc = qi // CHUNK
    kc = ki // CHUNK
    valid = (kc >= qc) & (kc <= qc + N_PREV_CHUNKS) & (ki + start >= BAND_PAST)
    tile = jnp.where(valid, b, NEG_INF)
    for j in range(NB):
        o_ref[0, 0, j] = tile[:, j * (TK // NB):(j + 1) * (TK // NB)]


def _bias_tiles(rel_bias, *, TQ, n_var, first_start, NB):
    H = rel_bias.shape[0]
    TK = BAND_PAST + TQ
    assert TQ + TK <= ROLL_W
    far = rel_bias[:, 2 * MAX_REL:]
    near = rel_bias[:, :1]
    by_m = jnp.concatenate([
        jnp.broadcast_to(far, (H, BAND_PAST - MAX_REL)),
        rel_bias[:, ::-1],
        jnp.broadcast_to(near, (H, ROLL_W - (BAND_PAST + MAX_REL) - 1)),
    ], axis=1)
    e = jnp.concatenate([by_m[:, :TK], jnp.broadcast_to(far, (H, ROLL_W - TK))], axis=1)
    e = e.reshape(H, 1, ROLL_W)
    return pl.pallas_call(
        functools.partial(_bias_kernel, TQ=TQ, TK=TK, NB=NB, first_start=first_start),
        grid=(n_var, H),
        in_specs=[pl.BlockSpec((1, 1, ROLL_W), lambda j, h: (h, 0, 0))],
        out_specs=pl.BlockSpec((1, 1, NB, TQ, TK // NB), lambda j, h: (j, h // 2, 0, h % 2, 0)),
        out_shape=jax.ShapeDtypeStruct((n_var, H // 2, NB, 2 * TQ, TK // NB), F32),
        compiler_params=_cparams(2),
        name="bias_tiles",
    )(e)


def _attn_kernel(q_ref, k_ref, v_ref, hk_ref, hv_ref, bias_ref, o_ref, kbuf, vbuf,
                 *, S, TQ, H, NB):
    t = pl.program_id(1)
    NS = kbuf.shape[1] // TQ
    cur = (t + NS - 1) % NS

    @pl.when(t == 0)
    def _():
        for s in range(S):
            kbuf[s, 0:BAND_PAST, :] = hk_ref[s]
            vbuf[s, 0:BAND_PAST, :] = hv_ref[s]

    row0 = pl.multiple_of(cur * TQ, TQ)
    for s in range(S):
        kbuf[s, pl.ds(row0, TQ), :] = k_ref[s]
        vbuf[s, pl.ds(row0, TQ), :] = v_ref[s]

    low = lax.broadcasted_iota(jnp.int32, (1, LANES), 1) < HEAD_DIM
    logical = [0] if NB == 1 else [(j + NS - 1 - cur) % NS for j in range(NS)]

    def cols(p):
        return slice(p * LANES, (p + 1) * LANES)

    def scores(s, p):
        q2 = q_ref[s, :, cols(p)]
        zero = jnp.zeros_like(q2)
        qq = jnp.concatenate([jnp.where(low, q2, zero), jnp.where(low, zero, q2)], axis=0)
        sc = lax.dot_general(qq, kbuf[s, :, cols(p)], (((1,), (1,)), ((), ())),
                             preferred_element_type=F32)
        bias = jnp.concatenate([bias_ref[0, p, logical[j]] for j in range(NB)], axis=-1)
        return sc + bias

    units = [(s, p) for s in range(S) for p in range(H // 2)]
    sc_next = scores(*units[0])
    for n, (s, p) in enumerate(units):
        sc = sc_next
        if n + 1 < len(units):
            sc_next = scores(*units[n + 1])
        mx = jnp.max(sc, axis=-1, keepdims=True)
        pr = jnp.exp(sc - mx)
        den = jnp.sum(pr, axis=-1, keepdims=True)
        oo = jnp.dot(pr.astype(BF16), vbuf[s, :, cols(p)], preferred_element_type=F32) / den
        o_ref[s, :, cols(p)] = jnp.where(low, oo[:TQ], oo[TQ:]).astype(BF16)


def _attention(q, kb, vb, hk, hv, bias, *, S, TQ):
    nseq, T, AD = q.shape
    H = AD // HEAD_DIM
    TK = BAND_PAST + TQ
    n_var, HP, NB, TQ2, BK = bias.shape
    assert HP * 2 == H and TQ2 == 2 * TQ
    assert NB * BK == TK and ((T == TQ and NB == 1) or (BK == TQ and TQ % LANES == 0))
    tile = pl.BlockSpec((S, TQ, AD), lambda i, t: (i, t, 0))
    hist = pl.BlockSpec((S, BAND_PAST, AD), lambda i, t: (i, 0, 0))
    return pl.pallas_call(
        functools.partial(_attn_kernel, S=S, TQ=TQ, H=H, NB=NB),
        grid=(nseq // S, T // TQ),
        in_specs=[tile, tile, tile, hist, hist,
                  pl.BlockSpec((1, HP, NB, TQ2, BK),
                               lambda i, t: (jnp.minimum(t, n_var - 1), 0, 0, 0, 0))],
        out_specs=tile,
        out_shape=jax.ShapeDtypeStruct((nseq, T, AD), BF16),
        scratch_shapes=[pltpu.VMEM((S, TK, AD), BF16), pltpu.VMEM((S, TK, AD), BF16)],
        compiler_params=_cparams(2),
        name="attn",
    )(q, kb, vb, hk, hv, bias)


def _outffn_kernel(x_ref, mod_ref, cy_ref, ao_ref, gc_ref, ga_ref, g2_ref,
                   wc_ref, wa_ref, wo_ref, wfi_ref, wfo_ref, y_ref, act,
                   *, S, R, D, F, FC):
    rows = S * R
    conv_out = jnp.dot(cy_ref[...].reshape(rows, -1), wc_ref[...], preferred_element_type=F32)
    attn_out = jnp.dot(ao_ref[...].reshape(rows, -1), wa_ref[...], preferred_element_type=F32)
    merged = (gc_ref[...].reshape(rows, D).astype(F32) * conv_out
              + ga_ref[...].reshape(rows, D).astype(F32) * attn_out)
    upd = jnp.dot(merged.astype(BF16), wo_ref[...], preferred_element_type=F32)
    gt1 = mod_ref[:, :, 2 * D:3 * D]
    x1 = x_ref[...] + gt1 * upd.reshape(S, R, D)

    ms = jnp.mean(x1 * x1, axis=-1, keepdims=True)
    xn = x1 * lax.rsqrt(ms + NORM_EPS) * g2_ref[...]
    sh2 = mod_ref[:, :, 3 * D:4 * D]
    sc2 = mod_ref[:, :, 4 * D:5 * D]
    h2 = (xn * (1.0 + sc2) + sh2).reshape(rows, D).astype(BF16)

    for c in range(F // FC):
        gate = jnp.dot(h2, wfi_ref[:, c * FC:(c + 1) * FC], preferred_element_type=F32)
        up = jnp.dot(h2, wfi_ref[:, F + c * FC:F + (c + 1) * FC], preferred_element_type=F32)
        act[:, c * FC:(c + 1) * FC] = (_silu(gate) * up).astype(BF16)
    ffn = jnp.dot(act[...], wfo_ref[...], preferred_element_type=F32)
    gt2 = mod_ref[:, :, 5 * D:6 * D]
    y_ref[...] = x1 + gt2 * ffn.reshape(S, R, D)


def _out_ffn(x, mod, cy, ao, gc, ga, g2, wc, wa, wo, wfi, wfo, *, S, R):
    nseq, T, D = x.shape
    F = wfo.shape[0]
    FC = 256
    tile = lambda c: pl.BlockSpec((S, R, c), lambda i, t: (i, t, 0))
    return pl.pallas_call(
        functools.partial(_outffn_kernel, S=S, R=R, D=D, F=F, FC=FC),
        grid=(nseq // S, T // R),
        in_specs=[tile(D),
                  pl.BlockSpec((S, 1, mod.shape[2]), lambda i, t: (i, 0, 0)),
                  tile(cy.shape[2]), tile(ao.shape[2]), tile(D), tile(D),
                  _resident(g2.shape), _resident(wc.shape), _resident(wa.shape),
                  _resident(wo.shape), _resident(wfi.shape), _resident(wfo.shape)],
        out_specs=tile(D),
        out_shape=jax.ShapeDtypeStruct((nseq, T, D), F32),
        scratch_shapes=[pltpu.VMEM((S * R, F), BF16)],
        compiler_params=_cparams(2),
        name="out_ffn",
    )(x, mod, cy, ao, gc, ga, g2, wc, wa, wo, wfi, wfo)


def _tiling(nseq, T, rows):
    R = min(T, rows)
    S = max(1, min(nseq, rows // R))
    assert T % R == 0 and nseq % S == 0
    return S, R


def _layer(x, mod, conv_hist, hk, hv, bias, p, *, TQ):
    nseq, T, D = x.shape
    S, R = _tiling(nseq, T, 512)
    u, q, kf, vf, kb, vb, gc, ga = _in_proj(
        x, mod, p["g1"], p["w_in"], p["qg"], p["kg"], p["gm"], S=S, R=R)
    cy = _conv(u, conv_hist, p["w_dw"], p["b_dw"], p["ln_g"], p["ln_b"], S=S, R=R)
    Sa, _ = _tiling(nseq, T, TQ * max(1, 512 // T))
    ao = _attention(q, kb, vb, hk, hv, bias, S=Sa, TQ=TQ)
    y = _out_ffn(x, mod, cy, ao, gc, ga, p["g2"], p["w_conv_out"], p["w_attn_out"],
                 p["w_o"], p["w_ffn_in"], p["w_ffn_out"], S=S, R=R)
    return y, u, kf, vf


def kernel(x_prompt, x_sample, c_prompt, c_sample, cache_conv, cache_k, cache_v, norm1_g, norm2_g, w_ada, b_ada, w_in, w_dw, b_dw, conv_ln_g, conv_ln_b, w_conv_out, q_norm_g, k_norm_g, rel_bias, w_attn_out, w_o, w_ffn_in, w_ffn_out):
    depth = norm1_g.shape[0]
    B, T, D = x_prompt.shape
    BS, TS, _ = x_sample.shape
    H, Dh = cache_k.shape[3], cache_k.shape[4]
    AD = H * Dh
    CD = w_dw.shape[2]
    cache_len = cache_k.shape[2]
    assert cache_len == BAND_PAST and TS == CHUNK and T % CHUNK == 0 and Dh == HEAD_DIM
    state_len = min(BAND_PAST, T)
    TQP = 128
    n_c = B + BS
    n_pad = -n_c % 8

    hid = jnp.arange(AD) // Dh
    gm = jnp.where(hid[:, None] == hid[None, :], 1.0 / Dh, 0.0).astype(BF16)

    xp, xs = x_prompt, x_sample
    outs = [[] for _ in range(6)]
    for l in range(depth):
        p = {
            "g1": norm1_g[l].reshape(1, D), "g2": norm2_g[l].reshape(1, D),
            "w_in": w_in[l].astype(BF16),
            "qg": jnp.tile(q_norm_g[l], H).reshape(1, AD), "kg": jnp.tile(k_norm_g[l], H).reshape(1, AD),
            "gm": gm,
            "w_dw": w_dw[l], "b_dw": b_dw[l].reshape(1, CD),
            "ln_g": conv_ln_g[l].reshape(1, CD), "ln_b": conv_ln_b[l].reshape(1, CD),
            "w_conv_out": w_conv_out[l].astype(BF16), "w_attn_out": w_attn_out[l].astype(BF16),
            "w_o": w_o[l].astype(BF16), "w_ffn_in": w_ffn_in[l].astype(BF16),
            "w_ffn_out": w_ffn_out[l].astype(BF16),
        }
        c_all = jnp.pad(jnp.concatenate([c_prompt, c_sample], axis=0), ((0, n_pad), (0, 0)))
        mod = _ada(c_all, w_ada[l], b_ada[l])
        mod_p = mod[:B].reshape(B, 1, -1)
        mod_s = mod[B:n_c].reshape(BS, 1, -1)

        bias_p = _bias_tiles(rel_bias[l], TQ=TQP, n_var=BAND_PAST // TQP + 1, first_start=0,
                             NB=BAND_PAST // TQP + 1)
        bias_s = _bias_tiles(rel_bias[l], TQ=TS, n_var=1, first_start=BAND_PAST, NB=1)

        zeros_c = jnp.zeros((B, CONV_K - 1, CD), F32)
        zeros_kv = jnp.zeros((B, BAND_PAST, AD), BF16)
        xp, u_p, kf_p, vf_p = _layer(xp, mod_p, zeros_c, zeros_kv, zeros_kv, bias_p, p, TQ=TQP)
        hk = cache_k[l].reshape(BS, cache_len, AD).astype(BF16)
        hv = cache_v[l].reshape(BS, cache_len, AD).astype(BF16)
        xs, u_s, kf_s, vf_s = _layer(xs, mod_s, cache_conv[l], hk, hv, bias_s, p, TQ=TS)

        outs[0].append(u_p[:, T - (CONV_K - 1):])
        outs[1].append(kf_p[:, T - state_len:].reshape(B, state_len, H, Dh))
        outs[2].append(vf_p[:, T - state_len:].reshape(B, state_len, H, Dh))
        outs[3].append(u_s[:, TS - (CONV_K - 1):])
        outs[4].append(kf_s.reshape(BS, TS, H, Dh))
        outs[5].append(vf_s.reshape(BS, TS, H, Dh))

    return (xp, xs) + tuple(jnp.stack(o) for o in outs)
```

```python
import functools
import math

import jax
import jax.numpy as jnp
from jax import lax
from jax.experimental import pallas as pl
from jax.experimental.pallas import tpu as pltpu

F32 = jnp.float32
BF16 = jnp.bfloat16

CHUNK = 64
N_PREV_CHUNKS = 8
BAND_PAST = N_PREV_CHUNKS * CHUNK
HEAD_DIM = 64
CONV_K = 31
MAX_REL = 128
NORM_EPS = 1e-6
NEG_INF = -1e30
LANES = 128
SUBLANES = 8
CONV_ROW_STRIDE = 4
HIST_ROWS = 32
ROLL_W = 1024
VMEM_LIMIT = 56 * 1024 * 1024


def _cparams(n_axes):
    return pltpu.CompilerParams(
        dimension_semantics=("arbitrary",) * n_axes, vmem_limit_bytes=VMEM_LIMIT)


def _resident(shape):
    nd = len(shape)
    return pl.BlockSpec(shape, lambda *_: (0,) * nd, pipeline_mode=pl.Buffered(1))


def _sigmoid(x):
    return 1.0 / (1.0 + jnp.exp(-x))


def _silu(x):
    return x * _sigmoid(x)


def _ada_kernel(c_ref, w_ref, b_ref, o_ref):
    a = _silu(c_ref[...]).astype(BF16)
    o_ref[...] = jnp.dot(a, w_ref[...].astype(BF16), preferred_element_type=F32) + b_ref[...]


def _ada(c_all, w_ada, b_ada):
    n, d = c_all.shape
    n_out = w_ada.shape[1]
    tn = 1024
    return pl.pallas_call(
        _ada_kernel,
        grid=(n_out // tn,),
        in_specs=[pl.BlockSpec((n, d), lambda j: (0, 0)),
                  pl.BlockSpec((d, tn), lambda j: (0, j)),
                  pl.BlockSpec((1, tn), lambda j: (0, j))],
        out_specs=pl.BlockSpec((n, tn), lambda j: (0, j)),
        out_shape=jax.ShapeDtypeStruct((n, n_out), F32),
        compiler_params=_cparams(1),
        name="ada",
    )(c_all, w_ada, b_ada.reshape(1, n_out))


CONV_LEAD = HIST_ROWS - (CONV_K - 1)
CONV_BLOCK_ROWS = SUBLANES * CONV_ROW_STRIDE
CONV_GROUP = 4


def _conv_carry_history(hist_ref, ext, *, S, R, NS):
    t = pl.program_id(1)

    @pl.when(t == 0)
    def _():
        for s in range(S):
            for c in range(NS):
                ext[s * NS + c, CONV_LEAD:HIST_ROWS, :] = hist_ref[s, :, c * LANES:(c + 1) * LANES]

    @pl.when(t > 0)
    def _():
        for s in range(S):
            for c in range(NS):
                ext[s * NS + c, 0:HIST_ROWS, :] = ext[s * NS + c, R:R + HIST_ROWS, :]


def _order_token(x):
    bits = pltpu.bitcast(x[:SUBLANES, :LANES].astype(F32), jnp.uint32)
    return lax.shift_right_logical(lax.shift_right_logical(bits, jnp.uint32(16)), jnp.uint32(16))


def _ordered_after(x, token):
    return pltpu.bitcast(pltpu.bitcast(x, jnp.uint32) | token, x.dtype)


def _conv_pieces(ext, ybuf, wdw_ref, *, S, R, NS):
    ST = CONV_ROW_STRIDE
    RB = CONV_BLOCK_ROWS
    G = min(CONV_GROUP, R // RB)

    def piece(slab, c, g0, after):
        accs = [None] * (G * ST)
        for j in range(CONV_K):
            tap = _ordered_after(jnp.broadcast_to(wdw_ref[c, j:j + 1, :], (SUBLANES, LANES)), after)
            for b in range(G):
                for q in range(ST):
                    start = g0 + b * RB + CONV_LEAD + q + j
                    term = ext[slab, pl.ds(start, SUBLANES, stride=ST), :] * tap
                    a = b * ST + q
                    accs[a] = term if accs[a] is None else accs[a] + term
        for b in range(G):
            for q in range(ST):
                ybuf[slab, pl.ds(g0 + b * RB + q, SUBLANES, stride=ST), :] = accs[b * ST + q]

    return [functools.partial(piece, s * NS + c, c, g0)
            for s in range(S) for c in range(NS) for g0 in range(0, R, RB * G)]


def _conv_finish(ybuf, bdw_ref, lg_ref, lb_ref, *, S, NS):
    y = jnp.concatenate(
        [jnp.concatenate([ybuf[s * NS + c] for c in range(NS)], axis=-1) for s in range(S)], axis=0)
    y = y + bdw_ref[...]
    mu = jnp.mean(y, axis=-1, keepdims=True)
    yc = y - mu
    var = jnp.mean(yc * yc, axis=-1, keepdims=True)
    yn = yc * lax.rsqrt(var + NORM_EPS) * lg_ref[...] + lb_ref[...]
    return _silu(yn).astype(BF16)


def _inproj_kernel(x_ref, mod_ref, g1_ref, w_ref, qg_ref, kg_ref, gm_ref,
                   hist_ref, wdw_ref, bdw_ref, lg_ref, lb_ref,
                   u_ref, cy_ref, q_ref, kf_ref, vf_ref, kb_ref, vb_ref, gc_ref, ga_ref,
                   ext, ybuf, *, S, R, D, CD, AD):
    rows = S * R
    NS = CD // LANES
    pieces = _conv_pieces(ext, ybuf, wdw_ref, S=S, R=R, NS=NS)
    _conv_carry_history(hist_ref, ext, S=S, R=R, NS=NS)

    x = x_ref[...]
    ms = jnp.mean(x * x, axis=-1, keepdims=True)
    xn = x * lax.rsqrt(ms + NORM_EPS) * g1_ref[...]
    sh = mod_ref[:, :, 0:D]
    sc = mod_ref[:, :, D:2 * D]
    h = (xn * (1.0 + sc) + sh).reshape(rows, D).astype(BF16)

    def sec(lo, width):
        return jnp.dot(h, w_ref[:, lo:lo + width], preferred_element_type=F32)

    def head_rms(z, g_ref):
        msq = jnp.dot((z * z).astype(BF16), gm_ref[...], preferred_element_type=F32)
        return z * lax.rsqrt(msq + NORM_EPS) * g_ref[...]

    u = sec(0, CD) * _sigmoid(sec(CD, CD))
    u_ref[...] = u.reshape(S, R, CD)
    for s in range(S):
        for c in range(NS):
            ext[s * NS + c, HIST_ROWS:HIST_ROWS + R, :] = u[s * R:(s + 1) * R, c * LANES:(c + 1) * LANES]

    o = 2 * CD

    def do_q():
        z = sec(o, AD)
        qn = head_rms(z, qg_ref) * (1.0 / math.sqrt(HEAD_DIM))
        q_ref[...] = qn.astype(BF16).reshape(S, R, AD)
        return z

    def do_k():
        z = sec(o + AD, AD)
        kn = head_rms(z, kg_ref)
        kf_ref[...] = kn.reshape(S, R, AD)
        kb_ref[...] = kn.astype(BF16).reshape(S, R, AD)
        return z

    def do_v():
        v = sec(o + 2 * AD, AD)
        vf_ref[...] = v.reshape(S, R, AD)
        vb_ref[...] = v.astype(BF16).reshape(S, R, AD)
        return v

    def do_gate(g_ref, lo, half):
        w = D // 2
        z = sec(lo + half * w, w)
        g_ref[:, :, half * w:(half + 1) * w] = _sigmoid(z).astype(BF16).reshape(S, R, w)
        return z

    og = o + 3 * AD
    units = [do_q, do_k, do_v] + [functools.partial(do_gate, g_ref, lo, half)
                                  for g_ref, lo in ((gc_ref, og), (ga_ref, og + D)) for half in (0, 1)]
    per_unit = -(-len(pieces) // len(units))
    for n, unit in enumerate(units):
        token = _order_token(unit())
        for piece in pieces[n * per_unit:(n + 1) * per_unit]:
            piece(token)
    cy_ref[...] = _conv_finish(ybuf, bdw_ref, lg_ref, lb_ref, S=S, NS=NS).reshape(S, R, CD)


def _in_proj(x, mod, g1, w_in, qg, kg, gm, hist, w_dw, b_dw, ln_g, ln_b, *, S, R):
    nseq, T, D = x.shape
    AD = qg.shape[1]
    CD = w_dw.shape[1]
    NS = CD // LANES
    assert R % (SUBLANES * CONV_ROW_STRIDE) == 0 and R >= HIST_ROWS
    w_slabs = jnp.pad(w_dw, ((0, -CONV_K % SUBLANES), (0, 0))).reshape(-1, NS, LANES).swapaxes(0, 1)
    grid = (nseq // S, T // R)
    tile = lambda c: pl.BlockSpec((S, R, c), lambda i, t: (i, t, 0))
    outs = [(CD, F32), (CD, BF16), (AD, BF16), (AD, F32), (AD, F32), (AD, BF16), (AD, BF16),
            (D, BF16), (D, BF16)]
    return pl.pallas_call(
        functools.partial(_inproj_kernel, S=S, R=R, D=D, CD=CD, AD=AD),
        grid=grid,
        in_specs=[tile(D),
                  pl.BlockSpec((S, 1, mod.shape[2]), lambda i, t: (i, 0, 0)),
                  _resident(g1.shape), _resident(w_in.shape), _resident(qg.shape),
                  _resident(kg.shape), _resident(gm.shape),
                  pl.BlockSpec((S, CONV_K - 1, CD), lambda i, t: (i, 0, 0)),
                  _resident(w_slabs.shape), _resident(b_dw.shape),
                  _resident(ln_g.shape), _resident(ln_b.shape)],
        out_specs=[tile(c) for c, _ in outs],
        out_shape=[jax.ShapeDtypeStruct((nseq, T, c), dt) for c, dt in outs],
        scratch_shapes=[pltpu.VMEM((S * NS, R + HIST_ROWS, LANES), F32),
                        pltpu.VMEM((S * NS, R, LANES), F32)],
        compiler_params=_cparams(2),
        name="in_proj",
    )(x, mod, g1, w_in, qg, kg, gm, hist, w_slabs, b_dw, ln_g, ln_b)


def _bias_kernel(e_ref, o_ref, *, TQ, TK, NB, first_start):
    start = first_start + pl.program_id(0) * TQ
    qi = lax.broadcasted_iota(jnp.int32, (TQ, TK), 0)
    ki = lax.broadcasted_iota(jnp.int32, (TQ, TK), 1)
    qc = qi // CHUNK
    kc = ki // CHUNK
    valid = (kc >= qc) & (kc <= qc + N_PREV_CHUNKS) & (ki + start >= BAND_PAST)
    BK = TK // NB
    for h in range(e_ref.shape[0]):
        row = jnp.broadcast_to(e_ref[h], (TQ, ROLL_W))
        b = pltpu.roll(row, 0, 1, stride=1, stride_axis=0)[:, :TK]
        tile = jnp.where(valid, b, NEG_INF)
        for j in range(NB):
            o_ref[0, h // 2, j, (h % 2) * TQ:(h % 2 + 1) * TQ, :] = tile[:, j * BK:(j + 1) * BK]


def _bias_tiles(rel_bias, *, TQ, n_var, first_start, NB):
    H = rel_bias.shape[0]
    TK = BAND_PAST + TQ
    assert TQ + TK <= ROLL_W
    far = rel_bias[:, 2 * MAX_REL:]
    near = rel_bias[:, :1]
    by_m = jnp.concatenate([
        jnp.broadcast_to(far, (H, BAND_PAST - MAX_REL)),
        rel_bias[:, ::-1],
        jnp.broadcast_to(near, (H, ROLL_W - (BAND_PAST + MAX_REL) - 1)),
    ], axis=1)
    e = jnp.concatenate([by_m[:, :TK], jnp.broadcast_to(far, (H, ROLL_W - TK))], axis=1)
    e = e.reshape(H, 1, ROLL_W)
    return pl.pallas_call(
        functools.partial(_bias_kernel, TQ=TQ, TK=TK, NB=NB, first_start=first_start),
        grid=(n_var,),
        in_specs=[pl.BlockSpec((H, 1, ROLL_W), lambda j: (0, 0, 0))],
        out_specs=pl.BlockSpec((1, H // 2, NB, 2 * TQ, TK // NB), lambda j: (j, 0, 0, 0, 0)),
        out_shape=jax.ShapeDtypeStruct((n_var, H // 2, NB, 2 * TQ, TK // NB), F32),
        compiler_params=_cparams(1),
        name="bias_tiles",
    )(e)


def _attn_kernel(q_ref, k_ref, v_ref, hk_ref, hv_ref, bias_ref, o_ref, kbuf, vbuf,
                 *, S, TQ, H, NB):
    t = pl.program_id(1)
    NS = kbuf.shape[1] // TQ
    cur = (t + NS - 1) % NS

    @pl.when(t == 0)
    def _():
        for s in range(S):
            kbuf[s, 0:BAND_PAST, :] = hk_ref[s]
            vbuf[s, 0:BAND_PAST, :] = hv_ref[s]

    row0 = pl.multiple_of(cur * TQ, TQ)
    for s in range(S):
        kbuf[s, pl.ds(row0, TQ), :] = k_ref[s]
        vbuf[s, pl.ds(row0, TQ), :] = v_ref[s]

    low = lax.broadcasted_iota(jnp.int32, (1, LANES), 1) < HEAD_DIM
    logical = [0] if NB == 1 else [(j + NS - 1 - cur) % NS for j in range(NS)]

    def cols(p):
        return slice(p * LANES, (p + 1) * LANES)

    def scores(s, p):
        q2 = q_ref[s, :, cols(p)]
        zero = jnp.zeros_like(q2)
        qq = jnp.concatenate([jnp.where(low, q2, zero), jnp.where(low, zero, q2)], axis=0)
        sc = lax.dot_general(qq, kbuf[s, :, cols(p)], (((1,), (1,)), ((), ())),
                             preferred_element_type=F32)
        bias = jnp.concatenate([bias_ref[0, p, logical[j]] for j in range(NB)], axis=-1)
        return sc + bias

    units = [(s, p) for s in range(S) for p in range(H // 2)]
    sc_next = scores(*units[0])
    for n, (s, p) in enumerate(units):
        sc = sc_next
        if n + 1 < len(units):
            sc_next = scores(*units[n + 1])
        mx = jnp.max(sc, axis=-1, keepdims=True)
        pr = jnp.exp(sc - mx)
        den = jnp.sum(pr, axis=-1, keepdims=True)
        oo = jnp.dot(pr.astype(BF16), vbuf[s, :, cols(p)], preferred_element_type=F32) / den
        o_ref[s, :, cols(p)] = jnp.where(low, oo[:TQ], oo[TQ:]).astype(BF16)


def _attention(q, kb, vb, hk, hv, bias, *, S, TQ):
    nseq, T, AD = q.shape
    H = AD // HEAD_DIM
    TK = BAND_PAST + TQ
    n_var, HP, NB, TQ2, BK = bias.shape
    assert HP * 2 == H and TQ2 == 2 * TQ
    assert NB * BK == TK and ((T == TQ and NB == 1) or (BK == TQ and TQ % LANES == 0))
    tile = pl.BlockSpec((S, TQ, AD), lambda i, t: (i, t, 0))
    hist = pl.BlockSpec((S, BAND_PAST, AD), lambda i, t: (i, 0, 0))
    return pl.pallas_call(
        functools.partial(_attn_kernel, S=S, TQ=TQ, H=H, NB=NB),
        grid=(nseq // S, T // TQ),
        in_specs=[tile, tile, tile, hist, hist,
                  pl.BlockSpec((1, HP, NB, TQ2, BK),
                               lambda i, t: (jnp.minimum(t, n_var - 1), 0, 0, 0, 0))],
        out_specs=tile,
        out_shape=jax.ShapeDtypeStruct((nseq, T, AD), BF16),
        scratch_shapes=[pltpu.VMEM((S, TK, AD), BF16), pltpu.VMEM((S, TK, AD), BF16)],
        compiler_params=_cparams(2),
        name="attn",
    )(q, kb, vb, hk, hv, bias)


def _outffn_kernel(x_ref, mod_ref, cy_ref, ao_ref, gc_ref, ga_ref, g2_ref,
                   wc_ref, wa_ref, wo_ref, wfi_ref, wfo_ref, y_ref, act,
                   *, S, R, D, F, FC):
    rows = S * R
    conv_out = jnp.dot(cy_ref[...].reshape(rows, -1), wc_ref[...], preferred_element_type=F32)
    attn_out = jnp.dot(ao_ref[...].reshape(rows, -1), wa_ref[...], preferred_element_type=F32)
    merged = (gc_ref[...].reshape(rows, D).astype(F32) * conv_out
              + ga_ref[...].reshape(rows, D).astype(F32) * attn_out)
    upd = jnp.dot(merged.astype(BF16), wo_ref[...], preferred_element_type=F32)
    gt1 = mod_ref[:, :, 2 * D:3 * D]
    x1 = x_ref[...] + gt1 * upd.reshape(S, R, D)

    ms = jnp.mean(x1 * x1, axis=-1, keepdims=True)
    xn = x1 * lax.rsqrt(ms + NORM_EPS) * g2_ref[...]
    sh2 = mod_ref[:, :, 3 * D:4 * D]
    sc2 = mod_ref[:, :, 4 * D:5 * D]
    h2 = (xn * (1.0 + sc2) + sh2).reshape(rows, D).astype(BF16)

    for c in range(F // FC):
        gate = jnp.dot(h2, wfi_ref[:, c * FC:(c + 1) * FC], preferred_element_type=F32)
        up = jnp.dot(h2, wfi_ref[:, F + c * FC:F + (c + 1) * FC], preferred_element_type=F32)
        act[:, c * FC:(c + 1) * FC] = (_silu(gate) * up).astype(BF16)
    ffn = jnp.dot(act[...], wfo_ref[...], preferred_element_type=F32)
    gt2 = mod_ref[:, :, 5 * D:6 * D]
    y_ref[...] = x1 + gt2 * ffn.reshape(S, R, D)


def _out_ffn(x, mod, cy, ao, gc, ga, g2, wc, wa, wo, wfi, wfo, *, S, R):
    nseq, T, D = x.shape
    F = wfo.shape[0]
    FC = 256
    tile = lambda c: pl.BlockSpec((S, R, c), lambda i, t: (i, t, 0))
    return pl.pallas_call(
        functools.partial(_outffn_kernel, S=S, R=R, D=D, F=F, FC=FC),
        grid=(nseq // S, T // R),
        in_specs=[tile(D),
                  pl.BlockSpec((S, 1, mod.shape[2]), lambda i, t: (i, 0, 0)),
                  tile(cy.shape[2]), tile(ao.shape[2]), tile(D), tile(D),
                  _resident(g2.shape), _resident(wc.shape), _resident(wa.shape),
                  _resident(wo.shape), _resident(wfi.shape), _resident(wfo.shape)],
        out_specs=tile(D),
        out_shape=jax.ShapeDtypeStruct((nseq, T, D), F32),
        scratch_shapes=[pltpu.VMEM((S * R, F), BF16)],
        compiler_params=_cparams(2),
        name="out_ffn",
    )(x, mod, cy, ao, gc, ga, g2, wc, wa, wo, wfi, wfo)


def _tiling(nseq, T, rows):
    R = min(T, rows)
    S = max(1, min(nseq, rows // R))
    assert T % R == 0 and nseq % S == 0
    return S, R


def _layer(x, mod, conv_hist, hk, hv, bias, p, *, TQ):
    nseq, T, D = x.shape
    S, R = _tiling(nseq, T, 512)
    u, cy, q, kf, vf, kb, vb, gc, ga = _in_proj(
        x, mod, p["g1"], p["w_in"], p["qg"], p["kg"], p["gm"],
        conv_hist, p["w_dw"], p["b_dw"], p["ln_g"], p["ln_b"], S=S, R=R)
    Sa, _ = _tiling(nseq, T, TQ * max(1, 512 // T))
    ao = _attention(q, kb, vb, hk, hv, bias, S=Sa, TQ=TQ)
    y = _out_ffn(x, mod, cy, ao, gc, ga, p["g2"], p["w_conv_out"], p["w_attn_out"],
                 p["w_o"], p["w_ffn_in"], p["w_ffn_out"], S=S, R=R)
    return y, u, kf, vf


def kernel(x_prompt, x_sample, c_prompt, c_sample, cache_conv, cache_k, cache_v, norm1_g, norm2_g, w_ada, b_ada, w_in, w_dw, b_dw, conv_ln_g, conv_ln_b, w_conv_out, q_norm_g, k_norm_g, rel_bias, w_attn_out, w_o, w_ffn_in, w_ffn_out):
    depth = norm1_g.shape[0]
    B, T, D = x_prompt.shape
    BS, TS, _ = x_sample.shape
    H, Dh = cache_k.shape[3], cache_k.shape[4]
    AD = H * Dh
    CD = w_dw.shape[2]
    cache_len = cache_k.shape[2]
    assert cache_len == BAND_PAST and TS == CHUNK and T % CHUNK == 0 and Dh == HEAD_DIM
    state_len = min(BAND_PAST, T)
    TQP = 128
    n_c = B + BS
    n_pad = -n_c % 8

    hid = jnp.arange(AD) // Dh
    gm = jnp.where(hid[:, None] == hid[None, :], 1.0 / Dh, 0.0).astype(BF16)

    xp, xs = x_prompt, x_sample
    outs = [[] for _ in range(6)]
    for l in range(depth):
        p = {
            "g1": norm1_g[l].reshape(1, D), "g2": norm2_g[l].reshape(1, D),
            "w_in": w_in[l].astype(BF16),
            "qg": jnp.tile(q_norm_g[l], H).reshape(1, AD), "kg": jnp.tile(k_norm_g[l], H).reshape(1, AD),
            "gm": gm,
            "w_dw": w_dw[l], "b_dw": b_dw[l].reshape(1, CD),
            "ln_g": conv_ln_g[l].reshape(1, CD), "ln_b": conv_ln_b[l].reshape(1, CD),
            "w_conv_out": w_conv_out[l].astype(BF16), "w_attn_out": w_attn_out[l].astype(BF16),
            "w_o": w_o[l].astype(BF16), "w_ffn_in": w_ffn_in[l].astype(BF16),
            "w_ffn_out": w_ffn_out[l].astype(BF16),
        }
        c_all = jnp.pad(jnp.concatenate([c_prompt, c_sample], axis=0), ((0, n_pad), (0, 0)))
        mod = _ada(c_all, w_ada[l], b_ada[l])
        mod_p = mod[:B].reshape(B, 1, -1)
        mod_s = mod[B:n_c].reshape(BS, 1, -1)

        bias_p = _bias_tiles(rel_bias[l], TQ=TQP, n_var=BAND_PAST // TQP + 1, first_start=0,
                             NB=BAND_PAST // TQP + 1)
        bias_s = _bias_tiles(rel_bias[l], TQ=TS, n_var=1, first_start=BAND_PAST, NB=1)

        zeros_c = jnp.zeros((B, CONV_K - 1, CD), F32)
        zeros_kv = jnp.zeros((B, BAND_PAST, AD), BF16)
        xp, u_p, kf_p, vf_p = _layer(xp, mod_p, zeros_c, zeros_kv, zeros_kv, bias_p, p, TQ=TQP)
        hk = cache_k[l].astype(BF16).reshape(BS, cache_len, AD)
        hv = cache_v[l].astype(BF16).reshape(BS, cache_len, AD)
        xs, u_s, kf_s, vf_s = _layer(xs, mod_s, cache_conv[l], hk, hv, bias_s, p, TQ=TS)

        outs[0].append(u_p[:, T - (CONV_K - 1):])
        outs[1].append(kf_p[:, T - state_len:].reshape(B, state_len, H, Dh))
        outs[2].append(vf_p[:, T - state_len:].reshape(B, state_len, H, Dh))
        outs[3].append(u_s[:, TS - (CONV_K - 1):])
        outs[4].append(kf_s.reshape(BS, TS, H, Dh))
        outs[5].append(vf_s.reshape(BS, TS, H, Dh))

    return (xp, xs) + tuple(jnp.stack(o) for o in outs)
```

```python
import functools
import math

import jax
import jax.numpy as jnp
from jax import lax
from jax.experimental import pallas as pl
from jax.experimental.pallas import tpu as pltpu

F32 = jnp.float32
BF16 = jnp.bfloat16

CHUNK = 64
N_PREV_CHUNKS = 8
BAND_PAST = N_PREV_CHUNKS * CHUNK
HEAD_DIM = 64
CONV_K = 31
MAX_REL = 128
NORM_EPS = 1e-6
NEG_INF = -1e30
LOG2_E = math.log2(math.e)
SCORE_SCALE = LOG2_E / math.sqrt(HEAD_DIM)
LANES = 128
SUBLANES = 8
CONV_ROW_STRIDE = 4
HIST_ROWS = 32
ROLL_W = 1024
VMEM_LIMIT = 56 * 1024 * 1024


def _cparams(n_axes):
    return pltpu.CompilerParams(
        dimension_semantics=("arbitrary",) * n_axes, vmem_limit_bytes=VMEM_LIMIT)


def _resident(shape):
    nd = len(shape)
    return pl.BlockSpec(shape, lambda *_: (0,) * nd, pipeline_mode=pl.Buffered(1))


def _sigmoid(x):
    return 1.0 / (1.0 + jnp.exp(-x))


def _silu(x):
    return x * _sigmoid(x)


def _ada_kernel(c_ref, w_ref, b_ref, o_ref):
    a = _silu(c_ref[...]).astype(BF16)
    o_ref[...] = jnp.dot(a, w_ref[...].astype(BF16), preferred_element_type=F32) + b_ref[...]


def _ada(c_all, w_ada, b_ada):
    n, d = c_all.shape
    n_out = w_ada.shape[1]
    tn = 1024
    return pl.pallas_call(
        _ada_kernel,
        grid=(n_out // tn,),
        in_specs=[pl.BlockSpec((n, d), lambda j: (0, 0)),
                  pl.BlockSpec((d, tn), lambda j: (0, j)),
                  pl.BlockSpec((1, tn), lambda j: (0, j))],
        out_specs=pl.BlockSpec((n, tn), lambda j: (0, j)),
        out_shape=jax.ShapeDtypeStruct((n, n_out), F32),
        compiler_params=_cparams(1),
        name="ada",
    )(c_all, w_ada, b_ada.reshape(1, n_out))


CONV_LEAD = HIST_ROWS - (CONV_K - 1)
CONV_BLOCK_ROWS = SUBLANES * CONV_ROW_STRIDE
CONV_GROUP = 4


def _order_token(x):
    bits = pltpu.bitcast(x[:SUBLANES, :LANES].astype(F32), jnp.uint32)
    return lax.shift_right_logical(lax.shift_right_logical(bits, jnp.uint32(16)), jnp.uint32(16))


def _ordered_after(x, token):
    return pltpu.bitcast(pltpu.bitcast(x, jnp.uint32) | token, x.dtype)


def _conv_pieces(ext, ybuf, wdw_ref, *, S, R, NS):
    ST = CONV_ROW_STRIDE
    RB = CONV_BLOCK_ROWS
    G = min(CONV_GROUP, R // RB)

    def piece(slab, c, g0, after=None):
        accs = [None] * (G * ST)
        for j in range(CONV_K):
            tap = jnp.broadcast_to(wdw_ref[c, j:j + 1, :], (SUBLANES, LANES))
            if after is not None:
                tap = _ordered_after(tap, after)
            for b in range(G):
                for q in range(ST):
                    start = g0 + b * RB + CONV_LEAD + q + j
                    term = ext[slab, pl.ds(start, SUBLANES, stride=ST), :] * tap
                    a = b * ST + q
                    accs[a] = term if accs[a] is None else accs[a] + term
        for b in range(G):
            for q in range(ST):
                ybuf[slab, pl.ds(g0 + b * RB + q, SUBLANES, stride=ST), :] = accs[b * ST + q]

    return [functools.partial(piece, s * NS + c, c, g0)
            for s in range(S) for c in range(NS) for g0 in range(0, R, RB * G)]


def _conv_finish(ybuf, bdw_ref, lg_ref, lb_ref, *, S, NS):
    y = jnp.concatenate(
        [jnp.concatenate([ybuf[s * NS + c] for c in range(NS)], axis=-1) for s in range(S)], axis=0)
    y = y + bdw_ref[...]
    mu = jnp.mean(y, axis=-1, keepdims=True)
    yc = y - mu
    var = jnp.mean(yc * yc, axis=-1, keepdims=True)
    yn = yc * lax.rsqrt(var + NORM_EPS) * lg_ref[...] + lb_ref[...]
    return _silu(yn).astype(BF16)


def _inproj_kernel(x_ref, mod_ref, g1_ref, w_ref, qg_ref, kg_ref, gm_ref,
                   u_ref, q_ref, kf_ref, vf_ref, kb_ref, vb_ref, gc_ref, ga_ref,
                   *, S, R, D, CD, AD):
    rows = S * R
    x = x_ref[...]
    ms = jnp.mean(x * x, axis=-1, keepdims=True)
    xn = x * lax.rsqrt(ms + NORM_EPS) * g1_ref[...]
    sh = mod_ref[:, :, 0:D]
    sc = mod_ref[:, :, D:2 * D]
    h = (xn * (1.0 + sc) + sh).reshape(rows, D).astype(BF16)

    def sec(lo, width):
        return jnp.dot(h, w_ref[:, lo:lo + width], preferred_element_type=F32)

    def head_rms(z, g_ref):
        msq = jnp.dot((z * z).astype(BF16), gm_ref[...], preferred_element_type=F32)
        return z * lax.rsqrt(msq + NORM_EPS) * g_ref[...]

    u = sec(0, CD) * _sigmoid(sec(CD, CD))
    u_ref[...] = u.reshape(S, R, CD)

    o = 2 * CD
    qn = head_rms(sec(o, AD), qg_ref) * SCORE_SCALE
    q_ref[...] = qn.astype(BF16).reshape(S, R, AD)
    kn = head_rms(sec(o + AD, AD), kg_ref)
    kf_ref[...] = kn.reshape(S, R, AD)
    kb_ref[...] = kn.astype(BF16).reshape(S, R, AD)
    v = sec(o + 2 * AD, AD)
    vf_ref[...] = v.reshape(S, R, AD)
    vb_ref[...] = v.astype(BF16).reshape(S, R, AD)
    o = o + 3 * AD
    gc_ref[...] = _sigmoid(sec(o, D)).astype(BF16).reshape(S, R, D)
    ga_ref[...] = _sigmoid(sec(o + D, D)).astype(BF16).reshape(S, R, D)


def _in_proj(x, mod, g1, w_in, qg, kg, gm, *, S, R):
    nseq, T, D = x.shape
    AD = qg.shape[1]
    CD = (w_in.shape[1] - 3 * AD - 2 * D) // 2
    grid = (nseq // S, T // R)
    tile = lambda c: pl.BlockSpec((S, R, c), lambda i, t: (i, t, 0))
    outs = [(CD, F32), (AD, BF16), (AD, F32), (AD, F32), (AD, BF16), (AD, BF16), (D, BF16), (D, BF16)]
    return pl.pallas_call(
        functools.partial(_inproj_kernel, S=S, R=R, D=D, CD=CD, AD=AD),
        grid=grid,
        in_specs=[tile(D),
                  pl.BlockSpec((S, 1, mod.shape[2]), lambda i, t: (i, 0, 0)),
                  _resident(g1.shape), _resident(w_in.shape), _resident(qg.shape),
                  _resident(kg.shape), _resident(gm.shape)],
        out_specs=[tile(c) for c, _ in outs],
        out_shape=[jax.ShapeDtypeStruct((nseq, T, c), dt) for c, dt in outs],
        compiler_params=_cparams(2),
        name="in_proj",
    )(x, mod, g1, w_in, qg, kg, gm)


def _bias_kernel(e_ref, o_ref, *, TQ, TK, NB, first_start):
    start = first_start + pl.program_id(0) * TQ
    qi = lax.broadcasted_iota(jnp.int32, (TQ, TK), 0)
    ki = lax.broadcasted_iota(jnp.int32, (TQ, TK), 1)
    qc = qi // CHUNK
    kc = ki // CHUNK
    valid = (kc >= qc) & (kc <= qc + N_PREV_CHUNKS) & (ki + start >= BAND_PAST)
    BK = TK // NB
    for h in range(e_ref.shape[0]):
        row = jnp.broadcast_to(e_ref[h], (TQ, ROLL_W))
        b = pltpu.roll(row, 0, 1, stride=1, stride_axis=0)[:, :TK]
        tile = jnp.where(valid, b * LOG2_E, NEG_INF)
        for j in range(NB):
            o_ref[0, h // 2, j, (h % 2) * TQ:(h % 2 + 1) * TQ, :] = tile[:, j * BK:(j + 1) * BK]


def _bias_tiles(rel_bias, *, TQ, n_var, first_start, NB):
    H = rel_bias.shape[0]
    TK = BAND_PAST + TQ
    assert TQ + TK <= ROLL_W
    far = rel_bias[:, 2 * MAX_REL:]
    near = rel_bias[:, :1]
    by_m = jnp.concatenate([
        jnp.broadcast_to(far, (H, BAND_PAST - MAX_REL)),
        rel_bias[:, ::-1],
        jnp.broadcast_to(near, (H, ROLL_W - (BAND_PAST + MAX_REL) - 1)),
    ], axis=1)
    e = jnp.concatenate([by_m[:, :TK], jnp.broadcast_to(far, (H, ROLL_W - TK))], axis=1)
    e = e.reshape(H, 1, ROLL_W)
    return pl.pallas_call(
        functools.partial(_bias_kernel, TQ=TQ, TK=TK, NB=NB, first_start=first_start),
        grid=(n_var,),
        in_specs=[pl.BlockSpec((H, 1, ROLL_W), lambda j: (0, 0, 0))],
        out_specs=pl.BlockSpec((1, H // 2, NB, 2 * TQ, TK // NB), lambda j: (j, 0, 0, 0, 0)),
        out_shape=jax.ShapeDtypeStruct((n_var, H // 2, NB, 2 * TQ, TK // NB), F32),
        compiler_params=_cparams(1),
        name="bias_tiles",
    )(e)


def _attn_kernel(q_ref, k_ref, v_ref, hk_ref, hv_ref, bias_ref, o_ref, kbuf, vbuf,
                 *, S, TQ, H, NB):
    t = pl.program_id(1)
    NS = kbuf.shape[1] // TQ
    cur = (t + NS - 1) % NS

    @pl.when(t == 0)
    def _():
        for s in range(S):
            kbuf[s, 0:BAND_PAST, :] = hk_ref[s]
            vbuf[s, 0:BAND_PAST, :] = hv_ref[s]

    row0 = pl.multiple_of(cur * TQ, TQ)
    for s in range(S):
        kbuf[s, pl.ds(row0, TQ), :] = k_ref[s]
        vbuf[s, pl.ds(row0, TQ), :] = v_ref[s]

    low = lax.broadcasted_iota(jnp.int32, (1, LANES), 1) < HEAD_DIM
    logical = [0] if NB == 1 else [(j + NS - 1 - cur) % NS for j in range(NS)]

    def cols(p):
        return slice(p * LANES, (p + 1) * LANES)

    def scores(s, p):
        q2 = q_ref[s, :, cols(p)]
        zero = jnp.zeros_like(q2)
        qq = jnp.concatenate([jnp.where(low, q2, zero), jnp.where(low, zero, q2)], axis=0)
        sc = lax.dot_general(qq, kbuf[s, :, cols(p)], (((1,), (1,)), ((), ())),
                             preferred_element_type=F32)
        bias = jnp.concatenate([bias_ref[0, p, logical[j]] for j in range(NB)], axis=-1)
        return sc + bias

    units = [(s, p) for s in range(S) for p in range(H // 2)]
    sc_next = scores(*units[0])
    for n, (s, p) in enumerate(units):
        sc = sc_next
        if n + 1 < len(units):
            sc_next = scores(*units[n + 1])
        mx = jnp.max(sc, axis=-1, keepdims=True)
        pr = jnp.exp2(sc - mx)
        den = jnp.sum(pr, axis=-1, keepdims=True)
        oo = jnp.dot(pr.astype(BF16), vbuf[s, :, cols(p)], preferred_element_type=F32) / den
        o_ref[s, :, cols(p)] = jnp.where(low, oo[:TQ], oo[TQ:]).astype(BF16)


def _attention(q, kb, vb, hk, hv, bias, *, S, TQ):
    nseq, T, AD = q.shape
    H = AD // HEAD_DIM
    TK = BAND_PAST + TQ
    n_var, HP, NB, TQ2, BK = bias.shape
    assert HP * 2 == H and TQ2 == 2 * TQ
    assert NB * BK == TK and ((T == TQ and NB == 1) or (BK == TQ and TQ % LANES == 0))
    tile = pl.BlockSpec((S, TQ, AD), lambda i, t: (i, t, 0))
    hist = pl.BlockSpec((S, BAND_PAST, AD), lambda i, t: (i, 0, 0))
    return pl.pallas_call(
        functools.partial(_attn_kernel, S=S, TQ=TQ, H=H, NB=NB),
        grid=(nseq // S, T // TQ),
        in_specs=[tile, tile, tile, hist, hist,
                  pl.BlockSpec((1, HP, NB, TQ2, BK),
                               lambda i, t: (jnp.minimum(t, n_var - 1), 0, 0, 0, 0))],
        out_specs=tile,
        out_shape=jax.ShapeDtypeStruct((nseq, T, AD), BF16),
        scratch_shapes=[pltpu.VMEM((S, TK, AD), BF16), pltpu.VMEM((S, TK, AD), BF16)],
        compiler_params=_cparams(2),
        name="attn",
    )(q, kb, vb, hk, hv, bias)


def _outffn_kernel(x_ref, mod_ref, u_ref, un_ref, hist_ref, wdw_ref, bdw_ref, lg_ref, lb_ref,
                   ao_ref, gc_ref, ga_ref, g2_ref,
                   wc_ref, wa_ref, wo_ref, wfi_ref, wfo_ref, y_ref, ext, ybuf, cybuf, act,
                   *, S, R, D, F, FC, lookahead):
    rows = S * R
    NS = u_ref.shape[2] // LANES
    t = pl.program_id(1)
    pieces = _conv_pieces(ext, ybuf, wdw_ref, S=S, R=R, NS=NS)
    finish = functools.partial(_conv_finish, ybuf, bdw_ref, lg_ref, lb_ref, S=S, NS=NS)

    def stage(tile_ref):
        for s in range(S):
            for c in range(NS):
                ext[s * NS + c, HIST_ROWS:HIST_ROWS + R, :] = tile_ref[s, :, c * LANES:(c + 1) * LANES]

    @pl.when(t == 0)
    def _():
        for s in range(S):
            for c in range(NS):
                ext[s * NS + c, CONV_LEAD:HIST_ROWS, :] = hist_ref[s, :, c * LANES:(c + 1) * LANES]
        stage(u_ref)
        for piece in pieces:
            piece()
        cybuf[0] = finish()

    cur = t % 2 if lookahead else 0
    cy = cybuf[cur]
    todo = []
    if lookahead:
        for s in range(S):
            for c in range(NS):
                ext[s * NS + c, 0:HIST_ROWS, :] = ext[s * NS + c, R:R + HIST_ROWS, :]
        stage(un_ref)

        def hand_over(_):
            cybuf[1 - cur] = finish()

        todo = list(pieces) + [hand_over]
    n_slots = 3 + F // FC + D // FC
    per_slot = -(-len(todo) // n_slots)

    def then_conv(value):
        for _ in range(min(per_slot, len(todo))):
            todo.pop(0)(_order_token(value))
        return value

    attn_out = then_conv(jnp.dot(ao_ref[...].reshape(rows, -1), wa_ref[...],
                                 preferred_element_type=F32))
    conv_out = then_conv(jnp.dot(cy, wc_ref[...], preferred_element_type=F32))
    merged = (gc_ref[...].reshape(rows, D).astype(F32) * conv_out
              + ga_ref[...].reshape(rows, D).astype(F32) * attn_out)
    upd = then_conv(jnp.dot(merged.astype(BF16), wo_ref[...], preferred_element_type=F32))
    gt1 = mod_ref[:, :, 2 * D:3 * D]
    x1 = x_ref[...] + gt1 * upd.reshape(S, R, D)

    ms = jnp.mean(x1 * x1, axis=-1, keepdims=True)
    xn = x1 * lax.rsqrt(ms + NORM_EPS) * g2_ref[...]
    sh2 = mod_ref[:, :, 3 * D:4 * D]
    sc2 = mod_ref[:, :, 4 * D:5 * D]
    h2 = (xn * (1.0 + sc2) + sh2).reshape(rows, D).astype(BF16)

    for c in range(F // FC):
        gate = then_conv(jnp.dot(h2, wfi_ref[:, c * FC:(c + 1) * FC], preferred_element_type=F32))
        up = jnp.dot(h2, wfi_ref[:, F + c * FC:F + (c + 1) * FC], preferred_element_type=F32)
        act[:, c * FC:(c + 1) * FC] = (_silu(gate) * up).astype(BF16)

    gt2 = mod_ref[:, :, 5 * D:6 * D]
    for c in range(D // FC):
        cols = slice(c * FC, (c + 1) * FC)
        ffn = then_conv(jnp.dot(act[...], wfo_ref[:, cols], preferred_element_type=F32))
        y_ref[:, :, cols] = x1[:, :, cols] + gt2[:, :, cols] * ffn.reshape(S, R, FC)
    assert not todo


def _out_ffn(x, mod, u, hist, w_dw, b_dw, ln_g, ln_b, ao, gc, ga, g2, wc, wa, wo, wfi, wfo, *, S, R):
    nseq, T, D = x.shape
    F = wfo.shape[0]
    FC = 256
    CD = u.shape[2]
    NS = CD // LANES
    n_t = T // R
    assert R % CONV_BLOCK_ROWS == 0 and R >= HIST_ROWS
    w_slabs = jnp.pad(w_dw, ((0, -CONV_K % SUBLANES), (0, 0))).reshape(-1, NS, LANES).swapaxes(0, 1)
    tile = lambda c: pl.BlockSpec((S, R, c), lambda i, t: (i, t, 0))
    return pl.pallas_call(
        functools.partial(_outffn_kernel, S=S, R=R, D=D, F=F, FC=FC, lookahead=n_t > 1),
        grid=(nseq // S, n_t),
        in_specs=[tile(D),
                  pl.BlockSpec((S, 1, mod.shape[2]), lambda i, t: (i, 0, 0)),
                  tile(CD),
                  pl.BlockSpec((S, R, CD), lambda i, t: (i, jnp.minimum(t + 1, n_t - 1), 0)),
                  pl.BlockSpec((S, CONV_K - 1, CD), lambda i, t: (i, 0, 0)),
                  _resident(w_slabs.shape), _resident(b_dw.shape),
                  _resident(ln_g.shape), _resident(ln_b.shape),
                  tile(ao.shape[2]), tile(D), tile(D),
                  _resident(g2.shape), _resident(wc.shape), _resident(wa.shape),
                  _resident(wo.shape), _resident(wfi.shape), _resident(wfo.shape)],
        out_specs=tile(D),
        out_shape=jax.ShapeDtypeStruct((nseq, T, D), F32),
        scratch_shapes=[pltpu.VMEM((S * NS, R + HIST_ROWS, LANES), F32),
                        pltpu.VMEM((S * NS, R, LANES), F32),
                        pltpu.VMEM((2, S * R, CD), BF16),
                        pltpu.VMEM((S * R, F), BF16)],
        compiler_params=_cparams(2),
        name="out_ffn",
    )(x, mod, u, u, hist, w_slabs, b_dw, ln_g, ln_b, ao, gc, ga, g2, wc, wa, wo, wfi, wfo)


def _tiling(nseq, T, rows):
    R = min(T, rows)
    S = max(1, min(nseq, rows // R))
    assert T % R == 0 and nseq % S == 0
    return S, R


def _layer(x, mod, conv_hist, hk, hv, bias, p, *, TQ):
    nseq, T, D = x.shape
    S, R = _tiling(nseq, T, 512)
    u, q, kf, vf, kb, vb, gc, ga = _in_proj(
        x, mod, p["g1"], p["w_in"], p["qg"], p["kg"], p["gm"], S=S, R=R)
    Sa, _ = _tiling(nseq, T, TQ * max(1, 512 // T))
    ao = _attention(q, kb, vb, hk, hv, bias, S=Sa, TQ=TQ)
    y = _out_ffn(x, mod, u, conv_hist, p["w_dw"], p["b_dw"], p["ln_g"], p["ln_b"], ao, gc, ga,
                 p["g2"], p["w_conv_out"], p["w_attn_out"], p["w_o"], p["w_ffn_in"], p["w_ffn_out"],
                 S=S, R=R)
    return y, u, kf, vf


def kernel(x_prompt, x_sample, c_prompt, c_sample, cache_conv, cache_k, cache_v, norm1_g, norm2_g, w_ada, b_ada, w_in, w_dw, b_dw, conv_ln_g, conv_ln_b, w_conv_out, q_norm_g, k_norm_g, rel_bias, w_attn_out, w_o, w_ffn_in, w_ffn_out):
    depth = norm1_g.shape[0]
    B, T, D = x_prompt.shape
    BS, TS, _ = x_sample.shape
    H, Dh = cache_k.shape[3], cache_k.shape[4]
    AD = H * Dh
    CD = w_dw.shape[2]
    cache_len = cache_k.shape[2]
    assert cache_len == BAND_PAST and TS == CHUNK and T % CHUNK == 0 and Dh == HEAD_DIM
    state_len = min(BAND_PAST, T)
    TQP = 256
    n_c = B + BS
    n_pad = -n_c % 8

    hid = jnp.arange(AD) // Dh
    gm = jnp.where(hid[:, None] == hid[None, :], 1.0 / Dh, 0.0).astype(BF16)

    xp, xs = x_prompt, x_sample
    outs = [[] for _ in range(6)]
    for l in range(depth):
        p = {
            "g1": norm1_g[l].reshape(1, D), "g2": norm2_g[l].reshape(1, D),
            "w_in": w_in[l].astype(BF16),
            "qg": jnp.tile(q_norm_g[l], H).reshape(1, AD), "kg": jnp.tile(k_norm_g[l], H).reshape(1, AD),
            "gm": gm,
            "w_dw": w_dw[l], "b_dw": b_dw[l].reshape(1, CD),
            "ln_g": conv_ln_g[l].reshape(1, CD), "ln_b": conv_ln_b[l].reshape(1, CD),
            "w_conv_out": w_conv_out[l].astype(BF16), "w_attn_out": w_attn_out[l].astype(BF16),
            "w_o": w_o[l].astype(BF16), "w_ffn_in": w_ffn_in[l].astype(BF16),
            "w_ffn_out": w_ffn_out[l].astype(BF16),
        }
        c_all = jnp.pad(jnp.concatenate([c_prompt, c_sample], axis=0), ((0, n_pad), (0, 0)))
        mod = _ada(c_all, w_ada[l], b_ada[l])
        mod_p = mod[:B].reshape(B, 1, -1)
        mod_s = mod[B:n_c].reshape(BS, 1, -1)

        bias_p = _bias_tiles(rel_bias[l], TQ=TQP, n_var=BAND_PAST // TQP + 1, first_start=0,
                             NB=BAND_PAST // TQP + 1)
        bias_s = _bias_tiles(rel_bias[l], TQ=TS, n_var=1, first_start=BAND_PAST, NB=1)

        zeros_c = jnp.zeros((B, CONV_K - 1, CD), F32)
        zeros_kv = jnp.zeros((B, BAND_PAST, AD), BF16)
        xp, u_p, kf_p, vf_p = _layer(xp, mod_p, zeros_c, zeros_kv, zeros_kv, bias_p, p, TQ=TQP)
        hk = cache_k[l].astype(BF16).reshape(BS, cache_len, AD)
        hv = cache_v[l].astype(BF16).reshape(BS, cache_len, AD)
        xs, u_s, kf_s, vf_s = _layer(xs, mod_s, cache_conv[l], hk, hv, bias_s, p, TQ=TS)

        outs[0].append(u_p[:, T - (CONV_K - 1):])
        outs[1].append(kf_p[:, T - state_len:].reshape(B, state_len, H, Dh))
        outs[2].append(vf_p[:, T - state_len:].reshape(B, state_len, H, Dh))
        outs[3].append(u_s[:, TS - (CONV_K - 1):])
        outs[4].append(kf_s.reshape(BS, TS, H, Dh))
        outs[5].append(vf_s.reshape(BS, TS, H, Dh))

    return (xp, xs) + tuple(jnp.stack(o) for o in outs)
```

```python
import functools
import math

import jax
import jax.numpy as jnp
from jax import lax
from jax.experimental import pallas as pl
from jax.experimental.pallas import tpu as pltpu

F32 = jnp.float32
BF16 = jnp.bfloat16

CHUNK = 64
N_PREV_CHUNKS = 8
BAND_PAST = N_PREV_CHUNKS * CHUNK
HEAD_DIM = 64
CONV_K = 31
MAX_REL = 128
NORM_EPS = 1e-6
NEG_INF = -1e30
LOG2_E = math.log2(math.e)
SCORE_SCALE = LOG2_E / math.sqrt(HEAD_DIM)
LANES = 128
SUBLANES = 8
CONV_ROW_STRIDE = 4
HIST_ROWS = 32
ROLL_W = 1024
VMEM_LIMIT = 56 * 1024 * 1024
ATTN_ROWS = 128


def _cparams(n_axes):
    return pltpu.CompilerParams(
        dimension_semantics=("arbitrary",) * n_axes, vmem_limit_bytes=VMEM_LIMIT)


def _resident(shape):
    nd = len(shape)
    return pl.BlockSpec(shape, lambda *_: (0,) * nd, pipeline_mode=pl.Buffered(1))


def _sigmoid(x):
    return 1.0 / (1.0 + jnp.exp(-x))


def _silu(x):
    return x * _sigmoid(x)


def _ada_kernel(c_ref, w_ref, b_ref, o_ref):
    a = _silu(c_ref[...]).astype(BF16)
    o_ref[...] = jnp.dot(a, w_ref[...].astype(BF16), preferred_element_type=F32) + b_ref[...]


def _ada(c_all, w_ada, b_ada):
    n, d = c_all.shape
    n_out = w_ada.shape[1]
    tn = 1024
    return pl.pallas_call(
        _ada_kernel,
        grid=(n_out // tn,),
        in_specs=[pl.BlockSpec((n, d), lambda j: (0, 0)),
                  pl.BlockSpec((d, tn), lambda j: (0, j)),
                  pl.BlockSpec((1, tn), lambda j: (0, j))],
        out_specs=pl.BlockSpec((n, tn), lambda j: (0, j)),
        out_shape=jax.ShapeDtypeStruct((n, n_out), F32),
        compiler_params=_cparams(1),
        name="ada",
    )(c_all, w_ada, b_ada.reshape(1, n_out))


CONV_LEAD = HIST_ROWS - (CONV_K - 1)
CONV_BLOCK_ROWS = SUBLANES * CONV_ROW_STRIDE
CONV_GROUP = 4


def _order_token(x):
    bits = pltpu.bitcast(x[:SUBLANES, :LANES].astype(F32), jnp.uint32)
    return lax.shift_right_logical(lax.shift_right_logical(bits, jnp.uint32(16)), jnp.uint32(16))


def _ordered_after(x, token):
    return pltpu.bitcast(pltpu.bitcast(x, jnp.uint32) | token, x.dtype)


def _conv_pieces(ext, ybuf, wdw_ref, *, S, R, NS):
    ST = CONV_ROW_STRIDE
    RB = CONV_BLOCK_ROWS
    G = min(CONV_GROUP, R // RB)

    def piece(slab, c, g0, after=None):
        accs = [None] * (G * ST)
        for j in range(CONV_K):
            tap = jnp.broadcast_to(wdw_ref[c, j:j + 1, :], (SUBLANES, LANES))
            if after is not None:
                tap = _ordered_after(tap, after)
            for b in range(G):
                for q in range(ST):
                    start = g0 + b * RB + CONV_LEAD + q + j
                    term = ext[slab, pl.ds(start, SUBLANES, stride=ST), :] * tap
                    a = b * ST + q
                    accs[a] = term if accs[a] is None else accs[a] + term
        for b in range(G):
            for q in range(ST):
                ybuf[slab, pl.ds(g0 + b * RB + q, SUBLANES, stride=ST), :] = accs[b * ST + q]

    return [functools.partial(piece, s * NS + c, c, g0)
            for s in range(S) for c in range(NS) for g0 in range(0, R, RB * G)]


def _conv_finish(ybuf, bdw_ref, lg_ref, lb_ref, *, S, NS):
    y = jnp.concatenate(
        [jnp.concatenate([ybuf[s * NS + c] for c in range(NS)], axis=-1) for s in range(S)], axis=0)
    y = y + bdw_ref[...]
    mu = jnp.mean(y, axis=-1, keepdims=True)
    yc = y - mu
    var = jnp.mean(yc * yc, axis=-1, keepdims=True)
    yn = yc * lax.rsqrt(var + NORM_EPS) * lg_ref[...] + lb_ref[...]
    return _silu(yn).astype(BF16)


def _inproj_kernel(x_ref, mod_ref, g1_ref, w_ref, qg_ref, kg_ref, gm_ref,
                   u_ref, q_ref, kf_ref, vf_ref, kb_ref, vb_ref, gc_ref, ga_ref,
                   *, S, R, D, CD, AD):
    rows = S * R
    x = x_ref[...]
    ms = jnp.mean(x * x, axis=-1, keepdims=True)
    xn = x * lax.rsqrt(ms + NORM_EPS) * g1_ref[...]
    sh = mod_ref[:, :, 0:D]
    sc = mod_ref[:, :, D:2 * D]
    h = (xn * (1.0 + sc) + sh).reshape(rows, D).astype(BF16)

    def sec(lo, width):
        return jnp.dot(h, w_ref[:, lo:lo + width], preferred_element_type=F32)

    def head_rms(z, g_ref):
        msq = jnp.dot((z * z).astype(BF16), gm_ref[...], preferred_element_type=F32)
        return z * lax.rsqrt(msq + NORM_EPS) * g_ref[...]

    u = sec(0, CD) * _sigmoid(sec(CD, CD))
    u_ref[...] = u.reshape(S, R, CD)

    o = 2 * CD
    qn = head_rms(sec(o, AD), qg_ref) * SCORE_SCALE
    q_ref[...] = qn.astype(BF16).reshape(S, R, AD)
    kn = head_rms(sec(o + AD, AD), kg_ref)
    kf_ref[...] = kn.reshape(S, R, AD)
    kb_ref[...] = kn.astype(BF16).reshape(S, R, AD)
    v = sec(o + 2 * AD, AD)
    vf_ref[...] = v.reshape(S, R, AD)
    vb_ref[...] = v.astype(BF16).reshape(S, R, AD)
    o = o + 3 * AD
    gc_ref[...] = _sigmoid(sec(o, D)).astype(BF16).reshape(S, R, D)
    ga_ref[...] = _sigmoid(sec(o + D, D)).astype(BF16).reshape(S, R, D)


def _in_proj(x, mod, g1, w_in, qg, kg, gm, *, S, R):
    nseq, T, D = x.shape
    AD = qg.shape[1]
    CD = (w_in.shape[1] - 3 * AD - 2 * D) // 2
    grid = (nseq // S, T // R)
    tile = lambda c: pl.BlockSpec((S, R, c), lambda i, t: (i, t, 0))
    outs = [(CD, F32), (AD, BF16), (AD, F32), (AD, F32), (AD, BF16), (AD, BF16), (D, BF16), (D, BF16)]
    return pl.pallas_call(
        functools.partial(_inproj_kernel, S=S, R=R, D=D, CD=CD, AD=AD),
        grid=grid,
        in_specs=[tile(D),
                  pl.BlockSpec((S, 1, mod.shape[2]), lambda i, t: (i, 0, 0)),
                  _resident(g1.shape), _resident(w_in.shape), _resident(qg.shape),
                  _resident(kg.shape), _resident(gm.shape)],
        out_specs=[tile(c) for c, _ in outs],
        out_shape=[jax.ShapeDtypeStruct((nseq, T, c), dt) for c, dt in outs],
        compiler_params=_cparams(2),
        name="in_proj",
    )(x, mod, g1, w_in, qg, kg, gm)


def _bias_kernel(e_ref, o_ref, *, TQ, TK, NB, first_start):
    start = first_start + pl.program_id(0) * TQ
    qi = lax.broadcasted_iota(jnp.int32, (TQ, TK), 0)
    ki = lax.broadcasted_iota(jnp.int32, (TQ, TK), 1)
    qc = qi // CHUNK
    kc = ki // CHUNK
    valid = (kc >= qc) & (kc <= qc + N_PREV_CHUNKS) & (ki + start >= BAND_PAST)
    BK = TK // NB
    for h in range(e_ref.shape[0]):
        row = jnp.broadcast_to(e_ref[h], (TQ, ROLL_W))
        b = pltpu.roll(row, 0, 1, stride=1, stride_axis=0)[:, :TK]
        tile = jnp.where(valid, b * LOG2_E, NEG_INF)
        for j in range(NB):
            o_ref[0, h // 2, j, (h % 2) * TQ:(h % 2 + 1) * TQ, :] = tile[:, j * BK:(j + 1) * BK]


def _bias_tiles(rel_bias, *, TQ, n_var, first_start, NB):
    H = rel_bias.shape[0]
    TK = BAND_PAST + TQ
    assert TQ + TK <= ROLL_W
    far = rel_bias[:, 2 * MAX_REL:]
    near = rel_bias[:, :1]
    by_m = jnp.concatenate([
        jnp.broadcast_to(far, (H, BAND_PAST - MAX_REL)),
        rel_bias[:, ::-1],
        jnp.broadcast_to(near, (H, ROLL_W - (BAND_PAST + MAX_REL) - 1)),
    ], axis=1)
    e = jnp.concatenate([by_m[:, :TK], jnp.broadcast_to(far, (H, ROLL_W - TK))], axis=1)
    e = e.reshape(H, 1, ROLL_W)
    return pl.pallas_call(
        functools.partial(_bias_kernel, TQ=TQ, TK=TK, NB=NB, first_start=first_start),
        grid=(n_var,),
        in_specs=[pl.BlockSpec((H, 1, ROLL_W), lambda j: (0, 0, 0))],
        out_specs=pl.BlockSpec((1, H // 2, NB, 2 * TQ, TK // NB), lambda j: (j, 0, 0, 0, 0)),
        out_shape=jax.ShapeDtypeStruct((n_var, H // 2, NB, 2 * TQ, TK // NB), F32),
        compiler_params=_cparams(1),
        name="bias_tiles",
    )(e)


def _attn_kernel(q_ref, k_ref, v_ref, hk_ref, hv_ref, bias_ref, o_ref, kbuf, vbuf,
                 *, S, TQ, H, NB):
    t = pl.program_id(1)
    NS = kbuf.shape[1] // TQ
    cur = (t + NS - 1) % NS

    @pl.when(t == 0)
    def _():
        for s in range(S):
            kbuf[s, 0:BAND_PAST, :] = hk_ref[s].reshape(BAND_PAST, -1).astype(BF16)
            vbuf[s, 0:BAND_PAST, :] = hv_ref[s].reshape(BAND_PAST, -1).astype(BF16)

    row0 = pl.multiple_of(cur * TQ, TQ)
    for s in range(S):
        kbuf[s, pl.ds(row0, TQ), :] = k_ref[s]
        vbuf[s, pl.ds(row0, TQ), :] = v_ref[s]

    low = lax.broadcasted_iota(jnp.int32, (1, LANES), 1) < HEAD_DIM
    logical = [0] if NB == 1 else [(j + NS - 1 - cur) % NS for j in range(NS)]

    def cols(p):
        return slice(p * LANES, (p + 1) * LANES)

    def scores(s, p):
        q2 = q_ref[s, :, cols(p)]
        zero = jnp.zeros_like(q2)
        qq = jnp.concatenate([jnp.where(low, q2, zero), jnp.where(low, zero, q2)], axis=0)
        sc = lax.dot_general(qq, kbuf[s, :, cols(p)], (((1,), (1,)), ((), ())),
                             preferred_element_type=F32)
        bias = jnp.concatenate([bias_ref[0, p, logical[j]] for j in range(NB)], axis=-1)
        return sc + bias

    units = [(s, p) for s in range(S) for p in range(H // 2)]
    sc_next = scores(*units[0])
    for n, (s, p) in enumerate(units):
        sc = sc_next
        if n + 1 < len(units):
            sc_next = scores(*units[n + 1])
        mx = jnp.max(sc, axis=-1, keepdims=True)
        pr = jnp.exp2(sc - mx)
        den = jnp.sum(pr, axis=-1, keepdims=True)
        oo = jnp.dot(pr.astype(BF16), vbuf[s, :, cols(p)], preferred_element_type=F32) / den
        o_ref[s, :, cols(p)] = jnp.where(low, oo[:TQ], oo[TQ:]).astype(BF16)


def _attention(q, kb, vb, hk, hv, bias, *, S, TQ):
    nseq, T, AD = q.shape
    H = AD // HEAD_DIM
    TK = BAND_PAST + TQ
    n_var, HP, NB, TQ2, BK = bias.shape
    assert HP * 2 == H and TQ2 == 2 * TQ
    assert NB * BK == TK and ((T == TQ and NB == 1) or (BK == TQ and TQ % LANES == 0))
    tile = pl.BlockSpec((S, TQ, AD), lambda i, t: (i, t, 0))
    hist = pl.BlockSpec((S, BAND_PAST, H, HEAD_DIM), lambda i, t: (i, 0, 0, 0))
    return pl.pallas_call(
        functools.partial(_attn_kernel, S=S, TQ=TQ, H=H, NB=NB),
        grid=(nseq // S, T // TQ),
        in_specs=[tile, tile, tile, hist, hist,
                  pl.BlockSpec((1, HP, NB, TQ2, BK),
                               lambda i, t: (jnp.minimum(t, n_var - 1), 0, 0, 0, 0))],
        out_specs=tile,
        out_shape=jax.ShapeDtypeStruct((nseq, T, AD), BF16),
        scratch_shapes=[pltpu.VMEM((S, TK, AD), BF16), pltpu.VMEM((S, TK, AD), BF16)],
        compiler_params=_cparams(2),
        name="attn",
    )(q, kb, vb, hk, hv, bias)


def _outffn_kernel(x_ref, mod_ref, u_ref, un_ref, hist_ref, wdw_ref, bdw_ref, lg_ref, lb_ref,
                   ao_ref, gc_ref, ga_ref, g2_ref,
                   wc_ref, wa_ref, wo_ref, wfi_ref, wfo_ref, y_ref, ext, ybuf, cybuf, act,
                   *, S, R, D, F, FC, lookahead):
    rows = S * R
    NS = u_ref.shape[2] // LANES
    t = pl.program_id(1)
    pieces = _conv_pieces(ext, ybuf, wdw_ref, S=S, R=R, NS=NS)
    finish = functools.partial(_conv_finish, ybuf, bdw_ref, lg_ref, lb_ref, S=S, NS=NS)

    def stage(tile_ref):
        for s in range(S):
            for c in range(NS):
                ext[s * NS + c, HIST_ROWS:HIST_ROWS + R, :] = tile_ref[s, :, c * LANES:(c + 1) * LANES]

    @pl.when(t == 0)
    def _():
        for s in range(S):
            for c in range(NS):
                ext[s * NS + c, CONV_LEAD:HIST_ROWS, :] = hist_ref[s, :, c * LANES:(c + 1) * LANES]
        stage(u_ref)
        for piece in pieces:
            piece()
        cybuf[0] = finish()

    cur = t % 2 if lookahead else 0
    cy = cybuf[cur]
    todo = []
    if lookahead:
        for s in range(S):
            for c in range(NS):
                ext[s * NS + c, 0:HIST_ROWS, :] = ext[s * NS + c, R:R + HIST_ROWS, :]
        stage(un_ref)

        def hand_over(_):
            cybuf[1 - cur] = finish()

        todo = list(pieces) + [hand_over]
    n_slots = 3 + F // FC + D // FC
    quota = [len(todo) // n_slots + (i < len(todo) % n_slots) for i in range(n_slots)]

    def then_conv(value):
        for _ in range(quota.pop(0)):
            todo.pop(0)(_order_token(value))
        return value

    attn_out = then_conv(jnp.dot(ao_ref[...].reshape(rows, -1), wa_ref[...],
                                 preferred_element_type=F32))
    conv_out = then_conv(jnp.dot(cy, wc_ref[...], preferred_element_type=F32))
    merged = (gc_ref[...].reshape(rows, D).astype(F32) * conv_out
              + ga_ref[...].reshape(rows, D).astype(F32) * attn_out)
    upd = then_conv(jnp.dot(merged.astype(BF16), wo_ref[...], preferred_element_type=F32))
    gt1 = mod_ref[:, :, 2 * D:3 * D]
    x1 = x_ref[...] + gt1 * upd.reshape(S, R, D)

    ms = jnp.mean(x1 * x1, axis=-1, keepdims=True)
    xn = x1 * lax.rsqrt(ms + NORM_EPS) * g2_ref[...]
    sh2 = mod_ref[:, :, 3 * D:4 * D]
    sc2 = mod_ref[:, :, 4 * D:5 * D]
    h2 = (xn * (1.0 + sc2) + sh2).reshape(rows, D).astype(BF16)

    for c in range(F // FC):
        gate = then_conv(jnp.dot(h2, wfi_ref[:, c * FC:(c + 1) * FC], preferred_element_type=F32))
        up = jnp.dot(h2, wfi_ref[:, F + c * FC:F + (c + 1) * FC], preferred_element_type=F32)
        act[:, c * FC:(c + 1) * FC] = (_silu(gate) * up).astype(BF16)

    gt2 = mod_ref[:, :, 5 * D:6 * D]
    for c in range(D // FC):
        cols = slice(c * FC, (c + 1) * FC)
        ffn = then_conv(jnp.dot(act[...], wfo_ref[:, cols], preferred_element_type=F32))
        y_ref[:, :, cols] = x1[:, :, cols] + gt2[:, :, cols] * ffn.reshape(S, R, FC)
    assert not todo


def _out_ffn(x, mod, u, hist, w_dw, b_dw, ln_g, ln_b, ao, gc, ga, g2, wc, wa, wo, wfi, wfo, *, S, R):
    nseq, T, D = x.shape
    F = wfo.shape[0]
    FC = 256
    CD = u.shape[2]
    NS = CD // LANES
    n_t = T // R
    assert R % CONV_BLOCK_ROWS == 0 and R >= HIST_ROWS
    w_slabs = jnp.pad(w_dw, ((0, -CONV_K % SUBLANES), (0, 0))).reshape(-1, NS, LANES).swapaxes(0, 1)
    tile = lambda c: pl.BlockSpec((S, R, c), lambda i, t: (i, t, 0))
    return pl.pallas_call(
        functools.partial(_outffn_kernel, S=S, R=R, D=D, F=F, FC=FC, lookahead=n_t > 1),
        grid=(nseq // S, n_t),
        in_specs=[tile(D),
                  pl.BlockSpec((S, 1, mod.shape[2]), lambda i, t: (i, 0, 0)),
                  tile(CD),
                  pl.BlockSpec((S, R, CD), lambda i, t: (i, jnp.minimum(t + 1, n_t - 1), 0)),
                  pl.BlockSpec((S, CONV_K - 1, CD), lambda i, t: (i, 0, 0)),
                  _resident(w_slabs.shape), _resident(b_dw.shape),
                  _resident(ln_g.shape), _resident(ln_b.shape),
                  tile(ao.shape[2]), tile(D), tile(D),
                  _resident(g2.shape), _resident(wc.shape), _resident(wa.shape),
                  _resident(wo.shape), _resident(wfi.shape), _resident(wfo.shape)],
        out_specs=tile(D),
        out_shape=jax.ShapeDtypeStruct((nseq, T, D), F32),
        scratch_shapes=[pltpu.VMEM((S * NS, R + HIST_ROWS, LANES), F32),
                        pltpu.VMEM((S * NS, R, LANES), F32),
                        pltpu.VMEM((2, S * R, CD), BF16),
                        pltpu.VMEM((S * R, F), BF16)],
        compiler_params=_cparams(2),
        name="out_ffn",
    )(x, mod, u, u, hist, w_slabs, b_dw, ln_g, ln_b, ao, gc, ga, g2, wc, wa, wo, wfi, wfo)


def _tiling(nseq, T, rows):
    R = min(T, rows)
    S = max(1, min(nseq, rows // R))
    assert T % R == 0 and nseq % S == 0
    return S, R


def _layer(x, mod, conv_hist, hk, hv, bias, p, *, TQ):
    nseq, T, D = x.shape
    S, R = _tiling(nseq, T, 512)
    u, q, kf, vf, kb, vb, gc, ga = _in_proj(
        x, mod, p["g1"], p["w_in"], p["qg"], p["kg"], p["gm"], S=S, R=R)
    Sa, _ = _tiling(nseq, T, max(TQ, ATTN_ROWS))
    ao = _attention(q, kb, vb, hk, hv, bias, S=Sa, TQ=TQ)
    y = _out_ffn(x, mod, u, conv_hist, p["w_dw"], p["b_dw"], p["ln_g"], p["ln_b"], ao, gc, ga,
                 p["g2"], p["w_conv_out"], p["w_attn_out"], p["w_o"], p["w_ffn_in"], p["w_ffn_out"],
                 S=S, R=R)
    return y, u, kf, vf


def kernel(x_prompt, x_sample, c_prompt, c_sample, cache_conv, cache_k, cache_v, norm1_g, norm2_g, w_ada, b_ada, w_in, w_dw, b_dw, conv_ln_g, conv_ln_b, w_conv_out, q_norm_g, k_norm_g, rel_bias, w_attn_out, w_o, w_ffn_in, w_ffn_out):
    depth = norm1_g.shape[0]
    B, T, D = x_prompt.shape
    BS, TS, _ = x_sample.shape
    H, Dh = cache_k.shape[3], cache_k.shape[4]
    AD = H * Dh
    CD = w_dw.shape[2]
    cache_len = cache_k.shape[2]
    assert cache_len == BAND_PAST and TS == CHUNK and T % CHUNK == 0 and Dh == HEAD_DIM
    state_len = min(BAND_PAST, T)
    TQP = 256
    n_c = B + BS
    n_pad = -n_c % 8

    hid = jnp.arange(AD) // Dh
    gm = jnp.where(hid[:, None] == hid[None, :], 1.0 / Dh, 0.0).astype(BF16)

    xp, xs = x_prompt, x_sample
    outs = [[] for _ in range(6)]
    for l in range(depth):
        p = {
            "g1": norm1_g[l].reshape(1, D), "g2": norm2_g[l].reshape(1, D),
            "w_in": w_in[l].astype(BF16),
            "qg": jnp.tile(q_norm_g[l], H).reshape(1, AD), "kg": jnp.tile(k_norm_g[l], H).reshape(1, AD),
            "gm": gm,
            "w_dw": w_dw[l], "b_dw": b_dw[l].reshape(1, CD),
            "ln_g": conv_ln_g[l].reshape(1, CD), "ln_b": conv_ln_b[l].reshape(1, CD),
            "w_conv_out": w_conv_out[l].astype(BF16), "w_attn_out": w_attn_out[l].astype(BF16),
            "w_o": w_o[l].astype(BF16), "w_ffn_in": w_ffn_in[l].astype(BF16),
            "w_ffn_out": w_ffn_out[l].astype(BF16),
        }
        c_all = jnp.pad(jnp.concatenate([c_prompt, c_sample], axis=0), ((0, n_pad), (0, 0)))
        mod = _ada(c_all, w_ada[l], b_ada[l])
        mod_p = mod[:B].reshape(B, 1, -1)
        mod_s = mod[B:n_c].reshape(BS, 1, -1)

        bias_p = _bias_tiles(rel_bias[l], TQ=TQP, n_var=BAND_PAST // TQP + 1, first_start=0,
                             NB=BAND_PAST // TQP + 1)
        bias_s = _bias_tiles(rel_bias[l], TQ=TS, n_var=1, first_start=BAND_PAST, NB=1)

        zeros_c = jnp.zeros((B, CONV_K - 1, CD), F32)
        zeros_kv = jnp.zeros((B, BAND_PAST, H, Dh), F32)
        xp, u_p, kf_p, vf_p = _layer(xp, mod_p, zeros_c, zeros_kv, zeros_kv, bias_p, p, TQ=TQP)
        xs, u_s, kf_s, vf_s = _layer(xs, mod_s, cache_conv[l], cache_k[l], cache_v[l], bias_s, p,
                                     TQ=TS)

        outs[0].append(u_p[:, T - (CONV_K - 1):])
        outs[1].append(kf_p[:, T - state_len:].reshape(B, state_len, H, Dh))
        outs[2].append(vf_p[:, T - state_len:].reshape(B, state_len, H, Dh))
        outs[3].append(u_s[:, TS - (CONV_K - 1):])
        outs[4].append(kf_s.reshape(BS, TS, H, Dh))
        outs[5].append(vf_s.reshape(BS, TS, H, Dh))

    return (xp, xs) + tuple(jnp.stack(o) for o in outs)
```

```python
import functools
import math

import jax
import jax.numpy as jnp
from jax import lax
from jax.experimental import pallas as pl
from jax.experimental.pallas import tpu as pltpu

F32 = jnp.float32
BF16 = jnp.bfloat16

CHUNK = 64
N_PREV_CHUNKS = 8
BAND_PAST = N_PREV_CHUNKS * CHUNK
HEAD_DIM = 64
CONV_K = 31
MAX_REL = 128
NORM_EPS = 1e-6
NEG_INF = -1e30
LOG2_E = math.log2(math.e)
SCORE_SCALE = LOG2_E / math.sqrt(HEAD_DIM)
LANES = 128
SUBLANES = 8
CONV_ROW_STRIDE = 4
HIST_ROWS = 32
ROLL_W = 1024
VMEM_LIMIT = 56 * 1024 * 1024
ATTN_ROWS = 256


def _cparams(n_axes):
    return pltpu.CompilerParams(
        dimension_semantics=("arbitrary",) * n_axes, vmem_limit_bytes=VMEM_LIMIT)


def _resident(shape):
    nd = len(shape)
    return pl.BlockSpec(shape, lambda *_: (0,) * nd, pipeline_mode=pl.Buffered(1))


def _sigmoid(x):
    return 1.0 / (1.0 + jnp.exp(-x))


def _silu(x):
    return x * _sigmoid(x)


def _ada_kernel(c_ref, w_ref, b_ref, o_ref):
    a = _silu(c_ref[...]).astype(BF16)
    o_ref[...] = jnp.dot(a, w_ref[...].astype(BF16), preferred_element_type=F32) + b_ref[...]


def _ada(c_all, w_ada, b_ada):
    n, d = c_all.shape
    n_out = w_ada.shape[1]
    tn = 1024
    return pl.pallas_call(
        _ada_kernel,
        grid=(n_out // tn,),
        in_specs=[pl.BlockSpec((n, d), lambda j: (0, 0)),
                  pl.BlockSpec((d, tn), lambda j: (0, j)),
                  pl.BlockSpec((1, tn), lambda j: (0, j))],
        out_specs=pl.BlockSpec((n, tn), lambda j: (0, j)),
        out_shape=jax.ShapeDtypeStruct((n, n_out), F32),
        compiler_params=_cparams(1),
        name="ada",
    )(c_all, w_ada, b_ada.reshape(1, n_out))


CONV_LEAD = HIST_ROWS - (CONV_K - 1)
CONV_BLOCK_ROWS = SUBLANES * CONV_ROW_STRIDE
CONV_GROUP = 4


def _order_token(x):
    bits = pltpu.bitcast(x[:SUBLANES, :LANES].astype(F32), jnp.uint32)
    return lax.shift_right_logical(lax.shift_right_logical(bits, jnp.uint32(16)), jnp.uint32(16))


def _ordered_after(x, token):
    return pltpu.bitcast(pltpu.bitcast(x, jnp.uint32) | token, x.dtype)


def _conv_pieces(ext, ybuf, wdw_ref, *, S, R, NS):
    ST = CONV_ROW_STRIDE
    RB = CONV_BLOCK_ROWS
    G = min(CONV_GROUP, R // RB)

    def piece(slab, c, g0, after=None):
        accs = [None] * (G * ST)
        for j in range(CONV_K):
            tap = jnp.broadcast_to(wdw_ref[c, j:j + 1, :], (SUBLANES, LANES))
            if after is not None:
                tap = _ordered_after(tap, after)
            for b in range(G):
                for q in range(ST):
                    start = g0 + b * RB + CONV_LEAD + q + j
                    term = ext[slab, pl.ds(start, SUBLANES, stride=ST), :] * tap
                    a = b * ST + q
                    accs[a] = term if accs[a] is None else accs[a] + term
        for b in range(G):
            for q in range(ST):
                ybuf[slab, pl.ds(g0 + b * RB + q, SUBLANES, stride=ST), :] = accs[b * ST + q]

    return [functools.partial(piece, s * NS + c, c, g0)
            for s in range(S) for c in range(NS) for g0 in range(0, R, RB * G)]


def _conv_finish(ybuf, bdw_ref, lg_ref, lb_ref, *, S, NS):
    y = jnp.concatenate(
        [jnp.concatenate([ybuf[s * NS + c] for c in range(NS)], axis=-1) for s in range(S)], axis=0)
    y = y + bdw_ref[...]
    mu = jnp.mean(y, axis=-1, keepdims=True)
    yc = y - mu
    var = jnp.mean(yc * yc, axis=-1, keepdims=True)
    yn = yc * lax.rsqrt(var + NORM_EPS) * lg_ref[...] + lb_ref[...]
    return _silu(yn).astype(BF16)


def _inproj_kernel(x_ref, mod_ref, g1_ref, w_ref, qg_ref, kg_ref, gm_ref,
                   u_ref, q_ref, kf_ref, vf_ref, kb_ref, vb_ref, gc_ref, ga_ref,
                   *, S, R, D, CD, AD):
    rows = S * R
    x = x_ref[...]
    ms = jnp.mean(x * x, axis=-1, keepdims=True)
    xn = x * lax.rsqrt(ms + NORM_EPS) * g1_ref[...]
    sh = mod_ref[:, :, 0:D]
    sc = mod_ref[:, :, D:2 * D]
    h = (xn * (1.0 + sc) + sh).reshape(rows, D).astype(BF16)

    def sec(lo, width):
        return jnp.dot(h, w_ref[:, lo:lo + width], preferred_element_type=F32)

    def head_rms(z, g_ref):
        msq = jnp.dot((z * z).astype(BF16), gm_ref[...], preferred_element_type=F32)
        return z * lax.rsqrt(msq + NORM_EPS) * g_ref[...]

    u = sec(0, CD) * _sigmoid(sec(CD, CD))
    u_ref[...] = u.reshape(S, R, CD)

    o = 2 * CD
    qn = head_rms(sec(o, AD), qg_ref) * SCORE_SCALE
    q_ref[...] = qn.astype(BF16).reshape(S, R, AD)
    kn = head_rms(sec(o + AD, AD), kg_ref)
    kf_ref[...] = kn.reshape(S, R, AD)
    kb_ref[...] = kn.astype(BF16).reshape(S, R, AD)
    v = sec(o + 2 * AD, AD)
    vf_ref[...] = v.reshape(S, R, AD)
    vb_ref[...] = v.astype(BF16).reshape(S, R, AD)
    o = o + 3 * AD
    gc_ref[...] = _sigmoid(sec(o, D)).astype(BF16).reshape(S, R, D)
    ga_ref[...] = _sigmoid(sec(o + D, D)).astype(BF16).reshape(S, R, D)


def _in_proj(x, mod, g1, w_in, qg, kg, gm, *, S, R):
    nseq, T, D = x.shape
    AD = qg.shape[1]
    CD = (w_in.shape[1] - 3 * AD - 2 * D) // 2
    grid = (nseq // S, T // R)
    tile = lambda c: pl.BlockSpec((S, R, c), lambda i, t: (i, t, 0))
    outs = [(CD, F32), (AD, BF16), (AD, F32), (AD, F32), (AD, BF16), (AD, BF16), (D, BF16), (D, BF16)]
    return pl.pallas_call(
        functools.partial(_inproj_kernel, S=S, R=R, D=D, CD=CD, AD=AD),
        grid=grid,
        in_specs=[tile(D),
                  pl.BlockSpec((S, 1, mod.shape[2]), lambda i, t: (i, 0, 0)),
                  _resident(g1.shape), _resident(w_in.shape), _resident(qg.shape),
                  _resident(kg.shape), _resident(gm.shape)],
        out_specs=[tile(c) for c, _ in outs],
        out_shape=[jax.ShapeDtypeStruct((nseq, T, c), dt) for c, dt in outs],
        compiler_params=_cparams(2),
        name="in_proj",
    )(x, mod, g1, w_in, qg, kg, gm)


def _bias_kernel(e_ref, o_ref, *, TQ, TK, NB, first_start):
    start = first_start + pl.program_id(0) * TQ
    qi = lax.broadcasted_iota(jnp.int32, (TQ, TK), 0)
    ki = lax.broadcasted_iota(jnp.int32, (TQ, TK), 1)
    qc = qi // CHUNK
    kc = ki // CHUNK
    valid = (kc >= qc) & (kc <= qc + N_PREV_CHUNKS) & (ki + start >= BAND_PAST)
    BK = TK // NB
    for h in range(e_ref.shape[0]):
        row = jnp.broadcast_to(e_ref[h], (TQ, ROLL_W))
        b = pltpu.roll(row, 0, 1, stride=1, stride_axis=0)[:, :TK]
        tile = jnp.where(valid, b * LOG2_E, NEG_INF)
        for j in range(NB):
            o_ref[0, h // 2, j, (h % 2) * TQ:(h % 2 + 1) * TQ, :] = tile[:, j * BK:(j + 1) * BK]


def _bias_tiles(rel_bias, *, TQ, n_var, first_start, NB):
    H = rel_bias.shape[0]
    TK = BAND_PAST + TQ
    assert TQ + TK <= ROLL_W
    far = rel_bias[:, 2 * MAX_REL:]
    near = rel_bias[:, :1]
    by_m = jnp.concatenate([
        jnp.broadcast_to(far, (H, BAND_PAST - MAX_REL)),
        rel_bias[:, ::-1],
        jnp.broadcast_to(near, (H, ROLL_W - (BAND_PAST + MAX_REL) - 1)),
    ], axis=1)
    e = jnp.concatenate([by_m[:, :TK], jnp.broadcast_to(far, (H, ROLL_W - TK))], axis=1)
    e = e.reshape(H, 1, ROLL_W)
    return pl.pallas_call(
        functools.partial(_bias_kernel, TQ=TQ, TK=TK, NB=NB, first_start=first_start),
        grid=(n_var,),
        in_specs=[pl.BlockSpec((H, 1, ROLL_W), lambda j: (0, 0, 0))],
        out_specs=pl.BlockSpec((1, H // 2, NB, 2 * TQ, TK // NB), lambda j: (j, 0, 0, 0, 0)),
        out_shape=jax.ShapeDtypeStruct((n_var, H // 2, NB, 2 * TQ, TK // NB), F32),
        compiler_params=_cparams(1),
        name="bias_tiles",
    )(e)


_NT = (((1,), (1,)), ((), ()))


def _lane_cols(p):
    return slice(p * LANES, (p + 1) * LANES)


def _first_head_lanes():
    return lax.broadcasted_iota(jnp.int32, (1, LANES), 1) < HEAD_DIM


def _stack_pair_queries(q2, low):
    zero = jnp.zeros_like(q2)
    return jnp.concatenate([jnp.where(low, q2, zero), jnp.where(low, zero, q2)], axis=0)


def _attend_units(units, scores, weighted_values, o_ref, TQ, low):
    sc_next = scores(*units[0])
    for n, (s, p) in enumerate(units):
        sc = sc_next
        if n + 1 < len(units):
            sc_next = scores(*units[n + 1])
        mx = jnp.max(sc, axis=-1, keepdims=True)
        pr = jnp.exp2(sc - mx)
        den = jnp.sum(pr, axis=-1, keepdims=True)
        oo = weighted_values(s, p, pr.astype(BF16)) / den
        o_ref[s, :, _lane_cols(p)] = jnp.where(low, oo[:TQ], oo[TQ:]).astype(BF16)


def _attn_ring_kernel(q_ref, k_ref, v_ref, bias_ref, o_ref, kbuf, vbuf, *, S, TQ, H):
    t = pl.program_id(1)
    NS = kbuf.shape[1] // TQ
    cur = (t + NS - 1) % NS

    @pl.when(t == 0)
    def _():
        kbuf[...] = jnp.zeros_like(kbuf)
        vbuf[...] = jnp.zeros_like(vbuf)

    row0 = pl.multiple_of(cur * TQ, TQ)
    for s in range(S):
        kbuf[s, pl.ds(row0, TQ), :] = k_ref[s]
        vbuf[s, pl.ds(row0, TQ), :] = v_ref[s]

    low = _first_head_lanes()
    logical = [(j + NS - 1 - cur) % NS for j in range(NS)]

    def scores(s, p):
        qq = _stack_pair_queries(q_ref[s, :, _lane_cols(p)], low)
        sc = lax.dot_general(qq, kbuf[s, :, _lane_cols(p)], _NT, preferred_element_type=F32)
        return sc + jnp.concatenate([bias_ref[0, p, logical[j]] for j in range(NS)], axis=-1)

    def weighted_values(s, p, pr):
        return jnp.dot(pr, vbuf[s, :, _lane_cols(p)], preferred_element_type=F32)

    units = [(s, p) for s in range(S) for p in range(H // 2)]
    _attend_units(units, scores, weighted_values, o_ref, TQ, low)


def _attn_cached_kernel(q_ref, k_ref, v_ref, hkt_ref, hvt_ref, bias_ref, o_ref, *, S, TQ, H):
    low = _first_head_lanes()

    def scores(s, p):
        qq = _stack_pair_queries(q_ref[s, :, _lane_cols(p)], low)
        past = jnp.dot(qq, hkt_ref[s, _lane_cols(p), :].astype(BF16), preferred_element_type=F32)
        own = lax.dot_general(qq, k_ref[s, :, _lane_cols(p)], _NT, preferred_element_type=F32)
        return jnp.concatenate([past, own], axis=-1) + bias_ref[0, p, 0]

    def weighted_values(s, p, pr):
        past = lax.dot_general(pr[:, :BAND_PAST], hvt_ref[s, _lane_cols(p), :].astype(BF16), _NT,
                               preferred_element_type=F32)
        own = jnp.dot(pr[:, BAND_PAST:], v_ref[s, :, _lane_cols(p)], preferred_element_type=F32)
        return past + own

    units = [(s, p) for s in range(S) for p in range(H // 2)]
    _attend_units(units, scores, weighted_values, o_ref, TQ, low)


def _attention(q, kb, vb, hkt, hvt, bias, *, S, TQ):
    nseq, T, AD = q.shape
    H = AD // HEAD_DIM
    TK = BAND_PAST + TQ
    n_var, HP, NB, TQ2, BK = bias.shape
    assert HP * 2 == H and TQ2 == 2 * TQ and NB * BK == TK
    tile = pl.BlockSpec((S, TQ, AD), lambda i, t: (i, t, 0))
    bias_spec = pl.BlockSpec((1, HP, NB, TQ2, BK),
                             lambda i, t: (jnp.minimum(t, n_var - 1), 0, 0, 0, 0))
    out_shape = jax.ShapeDtypeStruct((nseq, T, AD), BF16)
    if hkt is None:
        assert BK == TQ and TQ % LANES == 0
        return pl.pallas_call(
            functools.partial(_attn_ring_kernel, S=S, TQ=TQ, H=H),
            grid=(nseq // S, T // TQ),
            in_specs=[tile, tile, tile, bias_spec],
            out_specs=tile,
            out_shape=out_shape,
            scratch_shapes=[pltpu.VMEM((S, TK, AD), BF16), pltpu.VMEM((S, TK, AD), BF16)],
            compiler_params=_cparams(2),
            name="attn_ring",
        )(q, kb, vb, bias)
    assert T == TQ and NB == 1 and n_var == 1
    hist = pl.BlockSpec((S, AD, BAND_PAST), lambda i, t: (i, 0, 0))
    return pl.pallas_call(
        functools.partial(_attn_cached_kernel, S=S, TQ=TQ, H=H),
        grid=(nseq // S, 1),
        in_specs=[tile, tile, tile, hist, hist, bias_spec],
        out_specs=tile,
        out_shape=out_shape,
        compiler_params=_cparams(2),
        name="attn_cached",
    )(q, kb, vb, hkt, hvt, bias)


def _outffn_kernel(x_ref, mod_ref, u_ref, un_ref, hist_ref, wdw_ref, bdw_ref, lg_ref, lb_ref,
                   ao_ref, gc_ref, ga_ref, g2_ref,
                   wc_ref, wa_ref, wo_ref, wfi_ref, wfo_ref, y_ref, ext, ybuf, cybuf, act,
                   *, S, R, D, F, FC, lookahead):
    rows = S * R
    NS = u_ref.shape[2] // LANES
    t = pl.program_id(1)
    pieces = _conv_pieces(ext, ybuf, wdw_ref, S=S, R=R, NS=NS)
    finish = functools.partial(_conv_finish, ybuf, bdw_ref, lg_ref, lb_ref, S=S, NS=NS)

    def stage(tile_ref):
        for s in range(S):
            for c in range(NS):
                ext[s * NS + c, HIST_ROWS:HIST_ROWS + R, :] = tile_ref[s, :, c * LANES:(c + 1) * LANES]

    @pl.when(t == 0)
    def _():
        for s in range(S):
            for c in range(NS):
                ext[s * NS + c, CONV_LEAD:HIST_ROWS, :] = hist_ref[s, :, c * LANES:(c + 1) * LANES]
        stage(u_ref)
        for piece in pieces:
            piece()
        cybuf[0] = finish()

    cur = t % 2 if lookahead else 0
    cy = cybuf[cur]
    todo = []
    if lookahead:
        for s in range(S):
            for c in range(NS):
                ext[s * NS + c, 0:HIST_ROWS, :] = ext[s * NS + c, R:R + HIST_ROWS, :]
        stage(un_ref)

        def hand_over(_):
            cybuf[1 - cur] = finish()

        todo = list(pieces) + [hand_over]
    n_slots = 3 + F // FC + D // FC
    quota = [len(todo) // n_slots + (i < len(todo) % n_slots) for i in range(n_slots)]

    def then_conv(value):
        for _ in range(quota.pop(0)):
            todo.pop(0)(_order_token(value))
        return value

    attn_out = then_conv(jnp.dot(ao_ref[...].reshape(rows, -1), wa_ref[...],
                                 preferred_element_type=F32))
    conv_out = then_conv(jnp.dot(cy, wc_ref[...], preferred_element_type=F32))
    merged = (gc_ref[...].reshape(rows, D).astype(F32) * conv_out
              + ga_ref[...].reshape(rows, D).astype(F32) * attn_out)
    upd = then_conv(jnp.dot(merged.astype(BF16), wo_ref[...], preferred_element_type=F32))
    gt1 = mod_ref[:, :, 2 * D:3 * D]
    x1 = x_ref[...] + gt1 * upd.reshape(S, R, D)

    ms = jnp.mean(x1 * x1, axis=-1, keepdims=True)
    xn = x1 * lax.rsqrt(ms + NORM_EPS) * g2_ref[...]
    sh2 = mod_ref[:, :, 3 * D:4 * D]
    sc2 = mod_ref[:, :, 4 * D:5 * D]
    h2 = (xn * (1.0 + sc2) + sh2).reshape(rows, D).astype(BF16)

    for c in range(F // FC):
        gate = then_conv(jnp.dot(h2, wfi_ref[:, c * FC:(c + 1) * FC], preferred_element_type=F32))
        up = jnp.dot(h2, wfi_ref[:, F + c * FC:F + (c + 1) * FC], preferred_element_type=F32)
        act[:, c * FC:(c + 1) * FC] = (_silu(gate) * up).astype(BF16)

    gt2 = mod_ref[:, :, 5 * D:6 * D]
    for c in range(D // FC):
        cols = slice(c * FC, (c + 1) * FC)
        ffn = then_conv(jnp.dot(act[...], wfo_ref[:, cols], preferred_element_type=F32))
        y_ref[:, :, cols] = x1[:, :, cols] + gt2[:, :, cols] * ffn.reshape(S, R, FC)
    assert not todo


def _out_ffn(x, mod, u, hist, w_dw, b_dw, ln_g, ln_b, ao, gc, ga, g2, wc, wa, wo, wfi, wfo, *, S, R):
    nseq, T, D = x.shape
    F = wfo.shape[0]
    FC = 256
    CD = u.shape[2]
    NS = CD // LANES
    n_t = T // R
    assert R % CONV_BLOCK_ROWS == 0 and R >= HIST_ROWS
    w_slabs = jnp.pad(w_dw, ((0, -CONV_K % SUBLANES), (0, 0))).reshape(-1, NS, LANES).swapaxes(0, 1)
    tile = lambda c: pl.BlockSpec((S, R, c), lambda i, t: (i, t, 0))
    return pl.pallas_call(
        functools.partial(_outffn_kernel, S=S, R=R, D=D, F=F, FC=FC, lookahead=n_t > 1),
        grid=(nseq // S, n_t),
        in_specs=[tile(D),
                  pl.BlockSpec((S, 1, mod.shape[2]), lambda i, t: (i, 0, 0)),
                  tile(CD),
                  pl.BlockSpec((S, R, CD), lambda i, t: (i, jnp.minimum(t + 1, n_t - 1), 0)),
                  pl.BlockSpec((S, CONV_K - 1, CD), lambda i, t: (i, 0, 0)),
                  _resident(w_slabs.shape), _resident(b_dw.shape),
                  _resident(ln_g.shape), _resident(ln_b.shape),
                  tile(ao.shape[2]), tile(D), tile(D),
                  _resident(g2.shape), _resident(wc.shape), _resident(wa.shape),
                  _resident(wo.shape), _resident(wfi.shape), _resident(wfo.shape)],
        out_specs=tile(D),
        out_shape=jax.ShapeDtypeStruct((nseq, T, D), F32),
        scratch_shapes=[pltpu.VMEM((S * NS, R + HIST_ROWS, LANES), F32),
                        pltpu.VMEM((S * NS, R, LANES), F32),
                        pltpu.VMEM((2, S * R, CD), BF16),
                        pltpu.VMEM((S * R, F), BF16)],
        compiler_params=_cparams(2),
        name="out_ffn",
    )(x, mod, u, u, hist, w_slabs, b_dw, ln_g, ln_b, ao, gc, ga, g2, wc, wa, wo, wfi, wfo)


def _tiling(nseq, T, rows):
    R = min(T, rows)
    S = max(1, min(nseq, rows // R))
    assert T % R == 0 and nseq % S == 0
    return S, R


def _layer(x, mod, conv_hist, hk, hv, bias, p, *, TQ):
    nseq, T, D = x.shape
    S, R = _tiling(nseq, T, 512)
    u, q, kf, vf, kb, vb, gc, ga = _in_proj(
        x, mod, p["g1"], p["w_in"], p["qg"], p["kg"], p["gm"], S=S, R=R)
    Sa, _ = _tiling(nseq, T, max(TQ, ATTN_ROWS))
    ao = _attention(q, kb, vb, hk, hv, bias, S=Sa, TQ=TQ)
    y = _out_ffn(x, mod, u, conv_hist, p["w_dw"], p["b_dw"], p["ln_g"], p["ln_b"], ao, gc, ga,
                 p["g2"], p["w_conv_out"], p["w_attn_out"], p["w_o"], p["w_ffn_in"], p["w_ffn_out"],
                 S=S, R=R)
    return y, u, kf, vf


def kernel(x_prompt, x_sample, c_prompt, c_sample, cache_conv, cache_k, cache_v, norm1_g, norm2_g, w_ada, b_ada, w_in, w_dw, b_dw, conv_ln_g, conv_ln_b, w_conv_out, q_norm_g, k_norm_g, rel_bias, w_attn_out, w_o, w_ffn_in, w_ffn_out):
    depth = norm1_g.shape[0]
    B, T, D = x_prompt.shape
    BS, TS, _ = x_sample.shape
    H, Dh = cache_k.shape[3], cache_k.shape[4]
    AD = H * Dh
    CD = w_dw.shape[2]
    cache_len = cache_k.shape[2]
    assert cache_len == BAND_PAST and TS == CHUNK and T % CHUNK == 0 and Dh == HEAD_DIM
    state_len = min(BAND_PAST, T)
    TQP = 256
    n_c = B + BS
    n_pad = -n_c % 8

    hid = jnp.arange(AD) // Dh
    gm = jnp.where(hid[:, None] == hid[None, :], 1.0 / Dh, 0.0).astype(BF16)

    xp, xs = x_prompt, x_sample
    outs = [[] for _ in range(6)]
    for l in range(depth):
        p = {
            "g1": norm1_g[l].reshape(1, D), "g2": norm2_g[l].reshape(1, D),
            "w_in": w_in[l].astype(BF16),
            "qg": jnp.tile(q_norm_g[l], H).reshape(1, AD), "kg": jnp.tile(k_norm_g[l], H).reshape(1, AD),
            "gm": gm,
            "w_dw": w_dw[l], "b_dw": b_dw[l].reshape(1, CD),
            "ln_g": conv_ln_g[l].reshape(1, CD), "ln_b": conv_ln_b[l].reshape(1, CD),
            "w_conv_out": w_conv_out[l].astype(BF16), "w_attn_out": w_attn_out[l].astype(BF16),
            "w_o": w_o[l].astype(BF16), "w_ffn_in": w_ffn_in[l].astype(BF16),
            "w_ffn_out": w_ffn_out[l].astype(BF16),
        }
        c_all = jnp.pad(jnp.concatenate([c_prompt, c_sample], axis=0), ((0, n_pad), (0, 0)))
        mod = _ada(c_all, w_ada[l], b_ada[l])
        mod_p = mod[:B].reshape(B, 1, -1)
        mod_s = mod[B:n_c].reshape(BS, 1, -1)

        bias_p = _bias_tiles(rel_bias[l], TQ=TQP, n_var=BAND_PAST // TQP + 1, first_start=0,
                             NB=BAND_PAST // TQP + 1)
        bias_s = _bias_tiles(rel_bias[l], TQ=TS, n_var=1, first_start=BAND_PAST, NB=1)

        zeros_c = jnp.zeros((B, CONV_K - 1, CD), F32)
        xp, u_p, kf_p, vf_p = _layer(xp, mod_p, zeros_c, None, None, bias_p, p, TQ=TQP)
        hkt = jnp.transpose(cache_k[l], (0, 2, 3, 1)).reshape(BS, AD, cache_len)
        hvt = jnp.transpose(cache_v[l], (0, 2, 3, 1)).reshape(BS, AD, cache_len)
        xs, u_s, kf_s, vf_s = _layer(xs, mod_s, cache_conv[l], hkt, hvt, bias_s, p, TQ=TS)

        outs[0].append(u_p[:, T - (CONV_K - 1):])
        outs[1].append(kf_p[:, T - state_len:].reshape(B, state_len, H, Dh))
        outs[2].append(vf_p[:, T - state_len:].reshape(B, state_len, H, Dh))
        outs[3].append(u_s[:, TS - (CONV_K - 1):])
        outs[4].append(kf_s.reshape(BS, TS, H, Dh))
        outs[5].append(vf_s.reshape(BS, TS, H, Dh))

    return (xp, xs) + tuple(jnp.stack(o) for o in outs)
```

```python
import functools
import math

import jax
import jax.numpy as jnp
from jax import lax
from jax.experimental import pallas as pl
from jax.experimental.pallas import tpu as pltpu

F32 = jnp.float32
BF16 = jnp.bfloat16

CHUNK = 64
N_PREV_CHUNKS = 8
BAND_PAST = N_PREV_CHUNKS * CHUNK
HEAD_DIM = 64
CONV_K = 31
MAX_REL = 128
NORM_EPS = 1e-6
NEG_INF = -1e30
LOG2_E = math.log2(math.e)
SCORE_SCALE = LOG2_E / math.sqrt(HEAD_DIM)
LANES = 128
SUBLANES = 8
MXU_DIM = 256
CONV_ROW_STRIDE = 4
HIST_ROWS = 32
ROLL_W = 1024
VMEM_LIMIT = 56 * 1024 * 1024
IN_PROJ_ROWS = 1024
OUT_FFN_ROWS = 512
ATTN_ROWS = 256


def _cparams(n_axes):
    return pltpu.CompilerParams(
        dimension_semantics=("arbitrary",) * n_axes, vmem_limit_bytes=VMEM_LIMIT)


def _resident(shape):
    nd = len(shape)
    return pl.BlockSpec(shape, lambda *_: (0,) * nd, pipeline_mode=pl.Buffered(1))


def _sigmoid(x):
    return 1.0 / (1.0 + jnp.exp(-x))


def _silu(x):
    return x * _sigmoid(x)


def _ada_kernel(c_ref, w_ref, b_ref, o_ref):
    a = _silu(c_ref[...]).astype(BF16)
    o_ref[...] = jnp.dot(a, w_ref[...].astype(BF16), preferred_element_type=F32) + b_ref[...]


def _ada(c_all, w_ada, b_ada):
    n, d = c_all.shape
    n_out = w_ada.shape[1]
    tn = 1024
    return pl.pallas_call(
        _ada_kernel,
        grid=(n_out // tn,),
        in_specs=[pl.BlockSpec((n, d), lambda j: (0, 0)),
                  pl.BlockSpec((d, tn), lambda j: (0, j)),
                  pl.BlockSpec((1, tn), lambda j: (0, j))],
        out_specs=pl.BlockSpec((n, tn), lambda j: (0, j)),
        out_shape=jax.ShapeDtypeStruct((n, n_out), F32),
        compiler_params=_cparams(1),
        name="ada",
    )(c_all, w_ada, b_ada.reshape(1, n_out))


CONV_LEAD = HIST_ROWS - (CONV_K - 1)
CONV_BLOCK_ROWS = SUBLANES * CONV_ROW_STRIDE
CONV_GROUP = 4


def _order_token(x):
    bits = pltpu.bitcast(x[:SUBLANES, :LANES].astype(F32), jnp.uint32)
    return lax.shift_right_logical(lax.shift_right_logical(bits, jnp.uint32(16)), jnp.uint32(16))


def _ordered_after(x, token):
    return pltpu.bitcast(pltpu.bitcast(x, jnp.uint32) | token, x.dtype)


def _conv_pieces(ext, ybuf, wdw_ref, *, S, R, NS):
    ST = CONV_ROW_STRIDE
    RB = CONV_BLOCK_ROWS
    G = min(CONV_GROUP, R // RB)

    def piece(slab, c, g0, after=None):
        accs = [None] * (G * ST)
        for j in range(CONV_K):
            tap = jnp.broadcast_to(wdw_ref[c, j:j + 1, :], (SUBLANES, LANES))
            if after is not None:
                tap = _ordered_after(tap, after)
            for b in range(G):
                for q in range(ST):
                    start = g0 + b * RB + CONV_LEAD + q + j
                    term = ext[slab, pl.ds(start, SUBLANES, stride=ST), :] * tap
                    a = b * ST + q
                    accs[a] = term if accs[a] is None else accs[a] + term
        for b in range(G):
            for q in range(ST):
                ybuf[slab, pl.ds(g0 + b * RB + q, SUBLANES, stride=ST), :] = accs[b * ST + q]

    return [functools.partial(piece, s * NS + c, c, g0)
            for s in range(S) for c in range(NS) for g0 in range(0, R, RB * G)]


def _conv_finish(ybuf, bdw_ref, lg_ref, lb_ref, *, S, NS):
    y = jnp.concatenate(
        [jnp.concatenate([ybuf[s * NS + c] for c in range(NS)], axis=-1) for s in range(S)], axis=0)
    y = y + bdw_ref[...]
    mu = jnp.mean(y, axis=-1, keepdims=True)
    yc = y - mu
    var = jnp.mean(yc * yc, axis=-1, keepdims=True)
    yn = yc * lax.rsqrt(var + NORM_EPS) * lg_ref[...] + lb_ref[...]
    return _silu(yn).astype(BF16)


def _inproj_kernel(x_ref, mod_ref, g1_ref, w_ref, qg_ref, kg_ref, gm_ref,
                   u_ref, q_ref, kf_ref, vf_ref, kb_ref, vb_ref, gc_ref, ga_ref,
                   *, S, R, D, CD, AD):
    rows = S * R

    x = x_ref[...]
    ms = jnp.mean(x * x, axis=-1, keepdims=True)
    xn = x * lax.rsqrt(ms + NORM_EPS) * g1_ref[...]
    sh = mod_ref[:, :, 0:D]
    sc = mod_ref[:, :, D:2 * D]
    h = (xn * (1.0 + sc) + sh).reshape(rows, D).astype(BF16)

    def sec(lo, width):
        return jnp.dot(h, w_ref[:, lo:lo + width], preferred_element_type=F32)

    def head_rms(z, g_ref):
        zz = (z * z).astype(BF16)
        gw = gm_ref.shape[0]
        msq = jnp.concatenate(
            [jnp.dot(zz[:, c:c + gw], gm_ref[...], preferred_element_type=F32)
             for c in range(0, AD, gw)], axis=-1)
        return z * lax.rsqrt(msq + NORM_EPS) * g_ref[...]

    u = sec(0, CD) * _sigmoid(sec(CD, CD))
    u_ref[...] = u.reshape(S, R, CD)

    o = 2 * CD
    qn = head_rms(sec(o, AD), qg_ref) * SCORE_SCALE
    q_ref[...] = qn.astype(BF16).reshape(S, R, AD)
    kn = head_rms(sec(o + AD, AD), kg_ref)
    kf_ref[...] = kn.reshape(S, R, AD)
    kb_ref[...] = kn.astype(BF16).reshape(S, R, AD)
    v = sec(o + 2 * AD, AD)
    vf_ref[...] = v.reshape(S, R, AD)
    vb_ref[...] = v.astype(BF16).reshape(S, R, AD)
    o = o + 3 * AD
    gc_ref[...] = _sigmoid(sec(o, D)).astype(BF16).reshape(S, R, D)
    ga_ref[...] = _sigmoid(sec(o + D, D)).astype(BF16).reshape(S, R, D)


def _in_proj(x, mod, g1, w_in, qg, kg, gm, *, S, R):
    nseq, T, D = x.shape
    AD = qg.shape[1]
    CD = (w_in.shape[1] - 3 * AD - 2 * D) // 2
    grid = (nseq // S, T // R)
    tile = lambda c: pl.BlockSpec((S, R, c), lambda i, t: (i, t, 0))
    outs = [(CD, F32), (AD, BF16), (AD, F32), (AD, F32), (AD, BF16), (AD, BF16), (D, BF16), (D, BF16)]
    return pl.pallas_call(
        functools.partial(_inproj_kernel, S=S, R=R, D=D, CD=CD, AD=AD),
        grid=grid,
        in_specs=[tile(D),
                  pl.BlockSpec((S, 1, mod.shape[2]), lambda i, t: (i, 0, 0)),
                  _resident(g1.shape), _resident(w_in.shape), _resident(qg.shape),
                  _resident(kg.shape), _resident(gm.shape)],
        out_specs=[tile(c) for c, _ in outs],
        out_shape=[jax.ShapeDtypeStruct((nseq, T, c), dt) for c, dt in outs],
        compiler_params=_cparams(2),
        name="in_proj",
    )(x, mod, g1, w_in, qg, kg, gm)


def _bias_kernel(e_ref, o_ref, *, TQ, TK, NB, first_start):
    start = first_start + pl.program_id(0) * TQ
    qi = lax.broadcasted_iota(jnp.int32, (TQ, TK), 0)
    ki = lax.broadcasted_iota(jnp.int32, (TQ, TK), 1)
    qc = qi // CHUNK
    kc = ki // CHUNK
    valid = (kc >= qc) & (kc <= qc + N_PREV_CHUNKS) & (ki + start >= BAND_PAST)
    BK = TK // NB
    for h in range(e_ref.shape[0]):
        row = jnp.broadcast_to(e_ref[h], (TQ, ROLL_W))
        b = pltpu.roll(row, 0, 1, stride=1, stride_axis=0)[:, :TK]
        tile = jnp.where(valid, b * LOG2_E, NEG_INF)
        for j in range(NB):
            o_ref[0, h // 2, j, (h % 2) * TQ:(h % 2 + 1) * TQ, :] = tile[:, j * BK:(j + 1) * BK]


def _bias_tiles(rel_bias, *, TQ, n_var, first_start, NB):
    H = rel_bias.shape[0]
    TK = BAND_PAST + TQ
    assert TQ + TK <= ROLL_W
    far = rel_bias[:, 2 * MAX_REL:]
    near = rel_bias[:, :1]
    by_m = jnp.concatenate([
        jnp.broadcast_to(far, (H, BAND_PAST - MAX_REL)),
        rel_bias[:, ::-1],
        jnp.broadcast_to(near, (H, ROLL_W - (BAND_PAST + MAX_REL) - 1)),
    ], axis=1)
    e = jnp.concatenate([by_m[:, :TK], jnp.broadcast_to(far, (H, ROLL_W - TK))], axis=1)
    e = e.reshape(H, 1, ROLL_W)
    return pl.pallas_call(
        functools.partial(_bias_kernel, TQ=TQ, TK=TK, NB=NB, first_start=first_start),
        grid=(n_var,),
        in_specs=[pl.BlockSpec((H, 1, ROLL_W), lambda j: (0, 0, 0))],
        out_specs=pl.BlockSpec((1, H // 2, NB, 2 * TQ, TK // NB), lambda j: (j, 0, 0, 0, 0)),
        out_shape=jax.ShapeDtypeStruct((n_var, H // 2, NB, 2 * TQ, TK // NB), F32),
        compiler_params=_cparams(1),
        name="bias_tiles",
    )(e)


_NT = (((1,), (1,)), ((), ()))
SCORE_LOOKAHEAD = 2


def _lane_cols(p):
    return slice(p * LANES, (p + 1) * LANES)


def _first_head_lanes():
    return lax.broadcasted_iota(jnp.int32, (1, LANES), 1) < HEAD_DIM


def _stack_pair_queries(q2, low):
    zero = jnp.zeros_like(q2)
    return jnp.concatenate([jnp.where(low, q2, zero), jnp.where(low, zero, q2)], axis=0)


def _attend_units(units, scores, weighted_values, o_ref, TQ, low):
    ahead = [scores(*u) for u in units[:SCORE_LOOKAHEAD]]
    for n, (s, p) in enumerate(units):
        sc = ahead.pop(0)
        if n + SCORE_LOOKAHEAD < len(units):
            ahead.append(scores(*units[n + SCORE_LOOKAHEAD]))
        mx = jnp.max(sc, axis=-1, keepdims=True)
        pr = jnp.exp2(sc - mx)
        den = jnp.sum(pr, axis=-1, keepdims=True)
        oo = weighted_values(s, p, pr.astype(BF16)) / den
        o_ref[s, :, _lane_cols(p)] = jnp.where(low, oo[:TQ], oo[TQ:]).astype(BF16)


def _attn_ring_kernel(q_ref, k_ref, v_ref, bias_ref, o_ref, kbuf, vbuf, *, S, TQ, H):
    t = pl.program_id(1)
    NS = kbuf.shape[1] // TQ
    cur = (t + NS - 1) % NS

    @pl.when(t == 0)
    def _():
        kbuf[...] = jnp.zeros_like(kbuf)
        vbuf[...] = jnp.zeros_like(vbuf)

    row0 = pl.multiple_of(cur * TQ, TQ)
    for s in range(S):
        kbuf[s, pl.ds(row0, TQ), :] = k_ref[s]
        vbuf[s, pl.ds(row0, TQ), :] = v_ref[s]

    low = _first_head_lanes()
    logical = [(j + NS - 1 - cur) % NS for j in range(NS)]

    def scores(s, p):
        qq = _stack_pair_queries(q_ref[s, :, _lane_cols(p)], low)
        sc = lax.dot_general(qq, kbuf[s, :, _lane_cols(p)], _NT, preferred_element_type=F32)
        return sc + jnp.concatenate([bias_ref[0, p, logical[j]] for j in range(NS)], axis=-1)

    def weighted_values(s, p, pr):
        return jnp.dot(pr, vbuf[s, :, _lane_cols(p)], preferred_element_type=F32)

    units = [(s, p) for s in range(S) for p in range(H // 2)]
    _attend_units(units, scores, weighted_values, o_ref, TQ, low)


def _attn_cached_kernel(q_ref, k_ref, v_ref, hkt_ref, hvt_ref, bias_ref, o_ref, *, S, TQ, H):
    low = _first_head_lanes()

    def scores(s, p):
        qq = _stack_pair_queries(q_ref[s, :, _lane_cols(p)], low)
        past = jnp.dot(qq, hkt_ref[s, _lane_cols(p), :].astype(BF16), preferred_element_type=F32)
        own = lax.dot_general(qq, k_ref[s, :, _lane_cols(p)], _NT, preferred_element_type=F32)
        return jnp.concatenate([past, own], axis=-1) + bias_ref[0, p, 0]

    def weighted_values(s, p, pr):
        past = lax.dot_general(pr[:, :BAND_PAST], hvt_ref[s, _lane_cols(p), :].astype(BF16), _NT,
                               preferred_element_type=F32)
        own = jnp.dot(pr[:, BAND_PAST:], v_ref[s, :, _lane_cols(p)], preferred_element_type=F32)
        return past + own

    units = [(s, p) for s in range(S) for p in range(H // 2)]
    _attend_units(units, scores, weighted_values, o_ref, TQ, low)


def _attention(q, kb, vb, hkt, hvt, bias, *, S, TQ):
    nseq, T, AD = q.shape
    H = AD // HEAD_DIM
    TK = BAND_PAST + TQ
    n_var, HP, NB, TQ2, BK = bias.shape
    assert HP * 2 == H and TQ2 == 2 * TQ and NB * BK == TK
    tile = pl.BlockSpec((S, TQ, AD), lambda i, t: (i, t, 0))
    bias_spec = pl.BlockSpec((1, HP, NB, TQ2, BK),
                             lambda i, t: (jnp.minimum(t, n_var - 1), 0, 0, 0, 0))
    out_shape = jax.ShapeDtypeStruct((nseq, T, AD), BF16)
    if hkt is None:
        assert BK == TQ and TQ % LANES == 0
        return pl.pallas_call(
            functools.partial(_attn_ring_kernel, S=S, TQ=TQ, H=H),
            grid=(nseq // S, T // TQ),
            in_specs=[tile, tile, tile, bias_spec],
            out_specs=tile,
            out_shape=out_shape,
            scratch_shapes=[pltpu.VMEM((S, TK, AD), BF16), pltpu.VMEM((S, TK, AD), BF16)],
            compiler_params=_cparams(2),
            name="attn_ring",
        )(q, kb, vb, bias)
    assert T == TQ and NB == 1 and n_var == 1
    hist = pl.BlockSpec((S, AD, BAND_PAST), lambda i, t: (i, 0, 0))
    return pl.pallas_call(
        functools.partial(_attn_cached_kernel, S=S, TQ=TQ, H=H),
        grid=(nseq // S, 1),
        in_specs=[tile, tile, tile, hist, hist, bias_spec],
        out_specs=tile,
        out_shape=out_shape,
        compiler_params=_cparams(2),
        name="attn_cached",
    )(q, kb, vb, hkt, hvt, bias)


def _outffn_kernel(x_ref, mod_ref, u_ref, un_ref, hist_ref, wdw_ref, bdw_ref, lg_ref, lb_ref,
                   ao_ref, gc_ref, ga_ref, g2_ref,
                   wc_ref, wa_ref, wo_ref, wfi_ref, wfo_ref, y_ref, ext, ybuf, cybuf, act,
                   *, S, R, D, F, FC, lookahead):
    rows = S * R
    NS = u_ref.shape[2] // LANES
    t = pl.program_id(1)
    pieces = _conv_pieces(ext, ybuf, wdw_ref, S=S, R=R, NS=NS)
    finish = functools.partial(_conv_finish, ybuf, bdw_ref, lg_ref, lb_ref, S=S, NS=NS)

    def stage(tile_ref):
        for s in range(S):
            for c in range(NS):
                ext[s * NS + c, HIST_ROWS:HIST_ROWS + R, :] = tile_ref[s, :, c * LANES:(c + 1) * LANES]

    @pl.when(t == 0)
    def _():
        for s in range(S):
            for c in range(NS):
                ext[s * NS + c, CONV_LEAD:HIST_ROWS, :] = hist_ref[s, :, c * LANES:(c + 1) * LANES]
        stage(u_ref)
        for piece in pieces:
            piece()
        cybuf[0] = finish()

    cur = t % 2 if lookahead else 0
    cy = cybuf[cur]
    todo = []
    if lookahead:
        for s in range(S):
            for c in range(NS):
                ext[s * NS + c, 0:HIST_ROWS, :] = ext[s * NS + c, R:R + HIST_ROWS, :]
        stage(un_ref)

        def hand_over(_):
            cybuf[1 - cur] = finish()

        todo = list(pieces) + [hand_over]
    n_slots = 3 + F // FC + D // FC
    quota = [len(todo) // n_slots + (i < len(todo) % n_slots) for i in range(n_slots)]

    def then_conv(value):
        for _ in range(quota.pop(0)):
            todo.pop(0)(_order_token(value))
        return value

    attn_out = then_conv(jnp.dot(ao_ref[...].reshape(rows, -1), wa_ref[...],
                                 preferred_element_type=F32))
    conv_out = then_conv(jnp.dot(cy, wc_ref[...], preferred_element_type=F32))
    merged = (gc_ref[...].reshape(rows, D).astype(F32) * conv_out
              + ga_ref[...].reshape(rows, D).astype(F32) * attn_out)
    upd = then_conv(jnp.dot(merged.astype(BF16), wo_ref[...], preferred_element_type=F32))
    gt1 = mod_ref[:, :, 2 * D:3 * D]
    x1 = x_ref[...] + gt1 * upd.reshape(S, R, D)

    ms = jnp.mean(x1 * x1, axis=-1, keepdims=True)
    xn = x1 * lax.rsqrt(ms + NORM_EPS) * g2_ref[...]
    sh2 = mod_ref[:, :, 3 * D:4 * D]
    sc2 = mod_ref[:, :, 4 * D:5 * D]
    h2 = (xn * (1.0 + sc2) + sh2).reshape(rows, D).astype(BF16)

    for c in range(F // FC):
        gate = then_conv(jnp.dot(h2, wfi_ref[:, c * FC:(c + 1) * FC], preferred_element_type=F32))
        up = jnp.dot(h2, wfi_ref[:, F + c * FC:F + (c + 1) * FC], preferred_element_type=F32)
        act[:, c * FC:(c + 1) * FC] = (_silu(gate) * up).astype(BF16)

    gt2 = mod_ref[:, :, 5 * D:6 * D]
    for c in range(D // FC):
        cols = slice(c * FC, (c + 1) * FC)
        ffn = then_conv(jnp.dot(act[...], wfo_ref[:, cols], preferred_element_type=F32))
        y_ref[:, :, cols] = x1[:, :, cols] + gt2[:, :, cols] * ffn.reshape(S, R, FC)
    assert not todo


def _out_ffn(x, mod, u, hist, w_dw, b_dw, ln_g, ln_b, ao, gc, ga, g2, wc, wa, wo, wfi, wfo, *, S, R):
    nseq, T, D = x.shape
    F = wfo.shape[0]
    FC = 256
    CD = u.shape[2]
    NS = CD // LANES
    n_t = T // R
    assert R % CONV_BLOCK_ROWS == 0 and R >= HIST_ROWS
    w_slabs = jnp.pad(w_dw, ((0, -CONV_K % SUBLANES), (0, 0))).reshape(-1, NS, LANES).swapaxes(0, 1)
    tile = lambda c: pl.BlockSpec((S, R, c), lambda i, t: (i, t, 0))
    return pl.pallas_call(
        functools.partial(_outffn_kernel, S=S, R=R, D=D, F=F, FC=FC, lookahead=n_t > 1),
        grid=(nseq // S, n_t),
        in_specs=[tile(D),
                  pl.BlockSpec((S, 1, mod.shape[2]), lambda i, t: (i, 0, 0)),
                  tile(CD),
                  pl.BlockSpec((S, R, CD), lambda i, t: (i, jnp.minimum(t + 1, n_t - 1), 0)),
                  pl.BlockSpec((S, CONV_K - 1, CD), lambda i, t: (i, 0, 0)),
                  _resident(w_slabs.shape), _resident(b_dw.shape),
                  _resident(ln_g.shape), _resident(ln_b.shape),
                  tile(ao.shape[2]), tile(D), tile(D),
                  _resident(g2.shape), _resident(wc.shape), _resident(wa.shape),
                  _resident(wo.shape), _resident(wfi.shape), _resident(wfo.shape)],
        out_specs=tile(D),
        out_shape=jax.ShapeDtypeStruct((nseq, T, D), F32),
        scratch_shapes=[pltpu.VMEM((S * NS, R + HIST_ROWS, LANES), F32),
                        pltpu.VMEM((S * NS, R, LANES), F32),
                        pltpu.VMEM((2, S * R, CD), BF16),
                        pltpu.VMEM((S * R, F), BF16)],
        compiler_params=_cparams(2),
        name="out_ffn",
    )(x, mod, u, u, hist, w_slabs, b_dw, ln_g, ln_b, ao, gc, ga, g2, wc, wa, wo, wfi, wfo)


def _tiling(nseq, T, rows):
    R = min(T, rows)
    S = max(1, min(nseq, rows // R))
    assert T % R == 0 and nseq % S == 0
    return S, R


def _layer(x, mod, conv_hist, hk, hv, bias, p, *, TQ):
    nseq, T, D = x.shape
    Si, Ri = _tiling(nseq, T, IN_PROJ_ROWS)
    u, q, kf, vf, kb, vb, gc, ga = _in_proj(
        x, mod, p["g1"], p["w_in"], p["qg"], p["kg"], p["gm"], S=Si, R=Ri)
    S, R = _tiling(nseq, T, OUT_FFN_ROWS)
    Sa, _ = _tiling(nseq, T, max(TQ, ATTN_ROWS))
    ao = _attention(q, kb, vb, hk, hv, bias, S=Sa, TQ=TQ)
    y = _out_ffn(x, mod, u, conv_hist, p["w_dw"], p["b_dw"], p["ln_g"], p["ln_b"], ao, gc, ga,
                 p["g2"], p["w_conv_out"], p["w_attn_out"], p["w_o"], p["w_ffn_in"], p["w_ffn_out"],
                 S=S, R=R)
    return y, u, kf, vf


def kernel(x_prompt, x_sample, c_prompt, c_sample, cache_conv, cache_k, cache_v, norm1_g, norm2_g, w_ada, b_ada, w_in, w_dw, b_dw, conv_ln_g, conv_ln_b, w_conv_out, q_norm_g, k_norm_g, rel_bias, w_attn_out, w_o, w_ffn_in, w_ffn_out):
    depth = norm1_g.shape[0]
    B, T, D = x_prompt.shape
    BS, TS, _ = x_sample.shape
    H, Dh = cache_k.shape[3], cache_k.shape[4]
    AD = H * Dh
    CD = w_dw.shape[2]
    cache_len = cache_k.shape[2]
    assert cache_len == BAND_PAST and TS == CHUNK and T % CHUNK == 0 and Dh == HEAD_DIM
    state_len = min(BAND_PAST, T)
    TQP = 256
    n_c = B + BS
    n_pad = -n_c % 8

    hid = jnp.arange(MXU_DIM) // Dh
    gm = jnp.where(hid[:, None] == hid[None, :], 1.0 / Dh, 0.0).astype(BF16)

    xp, xs = x_prompt, x_sample
    outs = [[] for _ in range(6)]
    for l in range(depth):
        p = {
            "g1": norm1_g[l].reshape(1, D), "g2": norm2_g[l].reshape(1, D),
            "w_in": w_in[l].astype(BF16),
            "qg": jnp.tile(q_norm_g[l], H).reshape(1, AD), "kg": jnp.tile(k_norm_g[l], H).reshape(1, AD),
            "gm": gm,
            "w_dw": w_dw[l], "b_dw": b_dw[l].reshape(1, CD),
            "ln_g": conv_ln_g[l].reshape(1, CD), "ln_b": conv_ln_b[l].reshape(1, CD),
            "w_conv_out": w_conv_out[l].astype(BF16), "w_attn_out": w_attn_out[l].astype(BF16),
            "w_o": w_o[l].astype(BF16), "w_ffn_in": w_ffn_in[l].astype(BF16),
            "w_ffn_out": w_ffn_out[l].astype(BF16),
        }
        c_all = jnp.pad(jnp.concatenate([c_prompt, c_sample], axis=0), ((0, n_pad), (0, 0)))
        mod = _ada(c_all, w_ada[l], b_ada[l])
        mod_p = mod[:B].reshape(B, 1, -1)
        mod_s = mod[B:n_c].reshape(BS, 1, -1)

        bias_p = _bias_tiles(rel_bias[l], TQ=TQP, n_var=BAND_PAST // TQP + 1, first_start=0,
                             NB=BAND_PAST // TQP + 1)
        bias_s = _bias_tiles(rel_bias[l], TQ=TS, n_var=1, first_start=BAND_PAST, NB=1)

        zeros_c = jnp.zeros((B, CONV_K - 1, CD), F32)
        xp, u_p, kf_p, vf_p = _layer(xp, mod_p, zeros_c, None, None, bias_p, p, TQ=TQP)
        hkt = jnp.transpose(cache_k[l], (0, 2, 3, 1)).reshape(BS, AD, cache_len)
        hvt = jnp.transpose(cache_v[l], (0, 2, 3, 1)).reshape(BS, AD, cache_len)
        xs, u_s, kf_s, vf_s = _layer(xs, mod_s, cache_conv[l], hkt, hvt, bias_s, p, TQ=TS)

        outs[0].append(u_p[:, T - (CONV_K - 1):])
        outs[1].append(kf_p[:, T - state_len:].reshape(B, state_len, H, Dh))
        outs[2].append(vf_p[:, T - state_len:].reshape(B, state_len, H, Dh))
        outs[3].append(u_s[:, TS - (CONV_K - 1):])
        outs[4].append(kf_s.reshape(BS, TS, H, Dh))
        outs[5].append(vf_s.reshape(BS, TS, H, Dh))

    return (xp, xs) + tuple(jnp.stack(o) for o in outs)
```

```python
import functools
import math

import jax
import jax.numpy as jnp
from jax import lax
from jax.experimental import pallas as pl
from jax.experimental.pallas import tpu as pltpu

F32 = jnp.float32
BF16 = jnp.bfloat16

CHUNK = 64
N_PREV_CHUNKS = 8
BAND_PAST = N_PREV_CHUNKS * CHUNK
HEAD_DIM = 64
CONV_K = 31
MAX_REL = 128
NORM_EPS = 1e-6
NEG_INF = -1e30
LOG2_E = math.log2(math.e)
SCORE_SCALE = LOG2_E / math.sqrt(HEAD_DIM)
LANES = 128
SUBLANES = 8
MXU_DIM = 256
CONV_ROW_STRIDE = 4
HIST_ROWS = 32
ROLL_W = 1024
VMEM_LIMIT = 56 * 1024 * 1024
IN_PROJ_ROWS = 1024
OUT_FFN_ROWS = 512
ATTN_ROWS = 256


def _cparams(n_axes):
    return pltpu.CompilerParams(
        dimension_semantics=("arbitrary",) * n_axes, vmem_limit_bytes=VMEM_LIMIT)


def _resident(shape):
    nd = len(shape)
    return pl.BlockSpec(shape, lambda *_: (0,) * nd, pipeline_mode=pl.Buffered(1))


def _sigmoid(x):
    return 1.0 / (1.0 + jnp.exp(-x))


def _silu(x):
    return x * _sigmoid(x)


def _ada_kernel(c_ref, w_ref, b_ref, o_ref):
    a = _silu(c_ref[...]).astype(BF16)
    o_ref[...] = jnp.dot(a, w_ref[...].astype(BF16), preferred_element_type=F32) + b_ref[...]


def _ada(c_all, w_ada, b_ada):
    n, d = c_all.shape
    n_out = w_ada.shape[1]
    tn = 1024
    return pl.pallas_call(
        _ada_kernel,
        grid=(n_out // tn,),
        in_specs=[pl.BlockSpec((n, d), lambda j: (0, 0)),
                  pl.BlockSpec((d, tn), lambda j: (0, j)),
                  pl.BlockSpec((1, tn), lambda j: (0, j))],
        out_specs=pl.BlockSpec((n, tn), lambda j: (0, j)),
        out_shape=jax.ShapeDtypeStruct((n, n_out), F32),
        compiler_params=_cparams(1),
        name="ada",
    )(c_all, w_ada, b_ada.reshape(1, n_out))


CONV_LEAD = HIST_ROWS - (CONV_K - 1)
CONV_BLOCK_ROWS = SUBLANES * CONV_ROW_STRIDE
CONV_GROUP = 4


def _order_token(x):
    bits = pltpu.bitcast(x[:SUBLANES, :LANES].astype(F32), jnp.uint32)
    return lax.shift_right_logical(lax.shift_right_logical(bits, jnp.uint32(16)), jnp.uint32(16))


def _ordered_after(x, token):
    return pltpu.bitcast(pltpu.bitcast(x, jnp.uint32) | token, x.dtype)


def _conv_pieces(ext, ybuf, wdw_ref, *, S, R, NS):
    ST = CONV_ROW_STRIDE
    RB = CONV_BLOCK_ROWS
    G = min(CONV_GROUP, R // RB)

    def piece(slab, c, g0, after=None):
        accs = [None] * (G * ST)
        for j in range(CONV_K):
            tap = jnp.broadcast_to(wdw_ref[c, j:j + 1, :], (SUBLANES, LANES))
            if after is not None:
                tap = _ordered_after(tap, after)
            for b in range(G):
                for q in range(ST):
                    start = g0 + b * RB + CONV_LEAD + q + j
                    term = ext[slab, pl.ds(start, SUBLANES, stride=ST), :] * tap
                    a = b * ST + q
                    accs[a] = term if accs[a] is None else accs[a] + term
        for b in range(G):
            for q in range(ST):
                ybuf[slab, pl.ds(g0 + b * RB + q, SUBLANES, stride=ST), :] = accs[b * ST + q]

    return [functools.partial(piece, s * NS + c, c, g0)
            for s in range(S) for c in range(NS) for g0 in range(0, R, RB * G)]


def _conv_finish(ybuf, bdw_ref, lg_ref, lb_ref, *, S, NS):
    y = jnp.concatenate(
        [jnp.concatenate([ybuf[s * NS + c] for c in range(NS)], axis=-1) for s in range(S)], axis=0)
    y = y + bdw_ref[...]
    mu = jnp.mean(y, axis=-1, keepdims=True)
    yc = y - mu
    var = jnp.mean(yc * yc, axis=-1, keepdims=True)
    yn = yc * lax.rsqrt(var + NORM_EPS) * lg_ref[...] + lb_ref[...]
    return _silu(yn).astype(BF16)


def _inproj_kernel(x_ref, mod_ref, g1_ref, w_ref, qg_ref, kg_ref, gm_ref,
                   u_ref, q_ref, kf_ref, vf_ref, kb_ref, vb_ref, gc_ref, ga_ref,
                   *, S, R, D, CD, AD):
    rows = S * R

    x = x_ref[...]
    ms = jnp.mean(x * x, axis=-1, keepdims=True)
    xn = x * lax.rsqrt(ms + NORM_EPS) * g1_ref[...]
    sh = mod_ref[:, :, 0:D]
    sc = mod_ref[:, :, D:2 * D]
    h = (xn * (1.0 + sc) + sh).reshape(rows, D).astype(BF16)

    def sec(lo, width):
        return jnp.dot(h, w_ref[:, lo:lo + width], preferred_element_type=F32)

    def head_rms(z, g_ref):
        zz = (z * z).astype(BF16)
        gw = gm_ref.shape[0]
        msq = jnp.concatenate(
            [jnp.dot(zz[:, c:c + gw], gm_ref[...], preferred_element_type=F32)
             for c in range(0, AD, gw)], axis=-1)
        return z * lax.rsqrt(msq + NORM_EPS) * g_ref[...]

    u = sec(0, CD) * _sigmoid(sec(CD, CD))
    u_ref[...] = u.reshape(S, R, CD)

    o = 2 * CD
    qn = head_rms(sec(o, AD), qg_ref) * SCORE_SCALE
    q_ref[...] = qn.astype(BF16).reshape(S, R, AD)
    kn = head_rms(sec(o + AD, AD), kg_ref)
    kf_ref[...] = kn.reshape(S, R, AD)
    kb_ref[...] = kn.astype(BF16).reshape(S, R, AD)
    v = sec(o + 2 * AD, AD)
    vf_ref[...] = v.reshape(S, R, AD)
    vb_ref[...] = v.astype(BF16).reshape(S, R, AD)
    o = o + 3 * AD
    gc_ref[...] = _sigmoid(sec(o, D)).astype(BF16).reshape(S, R, D)
    ga_ref[...] = _sigmoid(sec(o + D, D)).astype(BF16).reshape(S, R, D)


def _in_proj(x, mod, g1, w_in, qg, kg, gm, *, S, R):
    nseq, T, D = x.shape
    AD = qg.shape[1]
    CD = (w_in.shape[1] - 3 * AD - 2 * D) // 2
    grid = (nseq // S, T // R)
    tile = lambda c: pl.BlockSpec((S, R, c), lambda i, t: (i, t, 0))
    last = lambda c: pl.BlockSpec((S, R, c), lambda i, t: (i, 0, 0))
    outs = [(CD, F32, tile, T), (AD, BF16, tile, T), (AD, F32, last, R), (AD, F32, last, R),
            (AD, BF16, tile, T), (AD, BF16, tile, T), (D, BF16, tile, T), (D, BF16, tile, T)]
    return pl.pallas_call(
        functools.partial(_inproj_kernel, S=S, R=R, D=D, CD=CD, AD=AD),
        grid=grid,
        in_specs=[tile(D),
                  pl.BlockSpec((S, 1, mod.shape[2]), lambda i, t: (i, 0, 0)),
                  _resident(g1.shape), _resident(w_in.shape), _resident(qg.shape),
                  _resident(kg.shape), _resident(gm.shape)],
        out_specs=[spec(c) for c, _, spec, _ in outs],
        out_shape=[jax.ShapeDtypeStruct((nseq, n, c), dt) for c, dt, _, n in outs],
        compiler_params=_cparams(2),
        name="in_proj",
    )(x, mod, g1, w_in, qg, kg, gm)


def _bias_kernel(e_ref, o_ref, *, TQ, TK, NB, first_start):
    start = first_start + pl.program_id(0) * TQ
    qi = lax.broadcasted_iota(jnp.int32, (TQ, TK), 0)
    ki = lax.broadcasted_iota(jnp.int32, (TQ, TK), 1)
    qc = qi // CHUNK
    kc = ki // CHUNK
    valid = (kc >= qc) & (kc <= qc + N_PREV_CHUNKS) & (ki + start >= BAND_PAST)
    BK = TK // NB
    for h in range(e_ref.shape[0]):
        row = jnp.broadcast_to(e_ref[h], (TQ, ROLL_W))
        b = pltpu.roll(row, 0, 1, stride=1, stride_axis=0)[:, :TK]
        tile = jnp.where(valid, b * LOG2_E, NEG_INF)
        for j in range(NB):
            o_ref[0, h // 2, j, (h % 2) * TQ:(h % 2 + 1) * TQ, :] = tile[:, j * BK:(j + 1) * BK]


def _bias_tiles(rel_bias, *, TQ, n_var, first_start, NB):
    H = rel_bias.shape[0]
    TK = BAND_PAST + TQ
    assert TQ + TK <= ROLL_W
    far = rel_bias[:, 2 * MAX_REL:]
    near = rel_bias[:, :1]
    by_m = jnp.concatenate([
        jnp.broadcast_to(far, (H, BAND_PAST - MAX_REL)),
        rel_bias[:, ::-1],
        jnp.broadcast_to(near, (H, ROLL_W - (BAND_PAST + MAX_REL) - 1)),
    ], axis=1)
    e = jnp.concatenate([by_m[:, :TK], jnp.broadcast_to(far, (H, ROLL_W - TK))], axis=1)
    e = e.reshape(H, 1, ROLL_W)
    return pl.pallas_call(
        functools.partial(_bias_kernel, TQ=TQ, TK=TK, NB=NB, first_start=first_start),
        grid=(n_var,),
        in_specs=[pl.BlockSpec((H, 1, ROLL_W), lambda j: (0, 0, 0))],
        out_specs=pl.BlockSpec((1, H // 2, NB, 2 * TQ, TK // NB), lambda j: (j, 0, 0, 0, 0)),
        out_shape=jax.ShapeDtypeStruct((n_var, H // 2, NB, 2 * TQ, TK // NB), F32),
        compiler_params=_cparams(1),
        name="bias_tiles",
    )(e)


_NT = (((1,), (1,)), ((), ()))
SCORE_LOOKAHEAD = 2


def _lane_cols(p):
    return slice(p * LANES, (p + 1) * LANES)


def _first_head_lanes():
    return lax.broadcasted_iota(jnp.int32, (1, LANES), 1) < HEAD_DIM


def _stack_pair_queries(q2, low):
    zero = jnp.zeros_like(q2)
    return jnp.concatenate([jnp.where(low, q2, zero), jnp.where(low, zero, q2)], axis=0)


def _attend_units(units, scores, weighted_values, o_ref, TQ, low):
    ahead = [scores(*u) for u in units[:SCORE_LOOKAHEAD]]
    for n, (s, p) in enumerate(units):
        sc = ahead.pop(0)
        if n + SCORE_LOOKAHEAD < len(units):
            ahead.append(scores(*units[n + SCORE_LOOKAHEAD]))
        mx = jnp.max(sc, axis=-1, keepdims=True)
        pr = jnp.exp2(sc - mx)
        den = jnp.sum(pr, axis=-1, keepdims=True)
        oo = weighted_values(s, p, pr.astype(BF16)) / den
        o_ref[s, :, _lane_cols(p)] = jnp.where(low, oo[:TQ], oo[TQ:]).astype(BF16)


def _attn_ring_kernel(q_ref, k_ref, v_ref, bias_ref, o_ref, kbuf, vbuf, *, S, TQ, H):
    t = pl.program_id(1)
    NS = kbuf.shape[1] // TQ
    cur = (t + NS - 1) % NS

    @pl.when(t == 0)
    def _():
        kbuf[...] = jnp.zeros_like(kbuf)
        vbuf[...] = jnp.zeros_like(vbuf)

    row0 = pl.multiple_of(cur * TQ, TQ)
    for s in range(S):
        kbuf[s, pl.ds(row0, TQ), :] = k_ref[s]
        vbuf[s, pl.ds(row0, TQ), :] = v_ref[s]

    low = _first_head_lanes()
    logical = [(j + NS - 1 - cur) % NS for j in range(NS)]

    def scores(s, p):
        qq = _stack_pair_queries(q_ref[s, :, _lane_cols(p)], low)
        sc = lax.dot_general(qq, kbuf[s, :, _lane_cols(p)], _NT, preferred_element_type=F32)
        return sc + jnp.concatenate([bias_ref[0, p, logical[j]] for j in range(NS)], axis=-1)

    def weighted_values(s, p, pr):
        return jnp.dot(pr, vbuf[s, :, _lane_cols(p)], preferred_element_type=F32)

    units = [(s, p) for s in range(S) for p in range(H // 2)]
    _attend_units(units, scores, weighted_values, o_ref, TQ, low)


def _attn_cached_kernel(q_ref, k_ref, v_ref, hkt_ref, hvt_ref, bias_ref, o_ref, *, S, TQ, H):
    low = _first_head_lanes()

    def scores(s, p):
        qq = _stack_pair_queries(q_ref[s, :, _lane_cols(p)], low)
        past = jnp.dot(qq, hkt_ref[s, _lane_cols(p), :].astype(BF16), preferred_element_type=F32)
        own = lax.dot_general(qq, k_ref[s, :, _lane_cols(p)], _NT, preferred_element_type=F32)
        return jnp.concatenate([past, own], axis=-1) + bias_ref[0, p, 0]

    def weighted_values(s, p, pr):
        past = lax.dot_general(pr[:, :BAND_PAST], hvt_ref[s, _lane_cols(p), :].astype(BF16), _NT,
                               preferred_element_type=F32)
        own = jnp.dot(pr[:, BAND_PAST:], v_ref[s, :, _lane_cols(p)], preferred_element_type=F32)
        return past + own

    units = [(s, p) for s in range(S) for p in range(H // 2)]
    _attend_units(units, scores, weighted_values, o_ref, TQ, low)


def _attention(q, kb, vb, hkt, hvt, bias, *, S, TQ):
    nseq, T, AD = q.shape
    H = AD // HEAD_DIM
    TK = BAND_PAST + TQ
    n_var, HP, NB, TQ2, BK = bias.shape
    assert HP * 2 == H and TQ2 == 2 * TQ and NB * BK == TK
    tile = pl.BlockSpec((S, TQ, AD), lambda i, t: (i, t, 0))
    bias_spec = pl.BlockSpec((1, HP, NB, TQ2, BK),
                             lambda i, t: (jnp.minimum(t, n_var - 1), 0, 0, 0, 0))
    out_shape = jax.ShapeDtypeStruct((nseq, T, AD), BF16)
    if hkt is None:
        assert BK == TQ and TQ % LANES == 0
        return pl.pallas_call(
            functools.partial(_attn_ring_kernel, S=S, TQ=TQ, H=H),
            grid=(nseq // S, T // TQ),
            in_specs=[tile, tile, tile, bias_spec],
            out_specs=tile,
            out_shape=out_shape,
            scratch_shapes=[pltpu.VMEM((S, TK, AD), BF16), pltpu.VMEM((S, TK, AD), BF16)],
            compiler_params=_cparams(2),
            name="attn_ring",
        )(q, kb, vb, bias)
    assert T == TQ and NB == 1 and n_var == 1
    hist = pl.BlockSpec((S, AD, BAND_PAST), lambda i, t: (i, 0, 0))
    return pl.pallas_call(
        functools.partial(_attn_cached_kernel, S=S, TQ=TQ, H=H),
        grid=(nseq // S, 1),
        in_specs=[tile, tile, tile, hist, hist, bias_spec],
        out_specs=tile,
        out_shape=out_shape,
        compiler_params=_cparams(2),
        name="attn_cached",
    )(q, kb, vb, hkt, hvt, bias)


def _outffn_kernel(x_ref, mod_ref, u_ref, un_ref, hist_ref, wdw_ref, bdw_ref, lg_ref, lb_ref,
                   ao_ref, gc_ref, ga_ref, g2_ref,
                   wc_ref, wa_ref, wo_ref, wfi_ref, wfo_ref, y_ref, ext, ybuf, cybuf, act,
                   *, S, R, D, F, FC, lookahead):
    rows = S * R
    NS = u_ref.shape[2] // LANES
    t = pl.program_id(1)
    pieces = _conv_pieces(ext, ybuf, wdw_ref, S=S, R=R, NS=NS)
    finish = functools.partial(_conv_finish, ybuf, bdw_ref, lg_ref, lb_ref, S=S, NS=NS)

    def stage(tile_ref):
        for s in range(S):
            for c in range(NS):
                ext[s * NS + c, HIST_ROWS:HIST_ROWS + R, :] = tile_ref[s, :, c * LANES:(c + 1) * LANES]

    @pl.when(t == 0)
    def _():
        for s in range(S):
            for c in range(NS):
                ext[s * NS + c, CONV_LEAD:HIST_ROWS, :] = hist_ref[s, :, c * LANES:(c + 1) * LANES]
        stage(u_ref)
        for piece in pieces:
            piece()
        cybuf[0] = finish()

    cur = t % 2 if lookahead else 0
    cy = cybuf[cur]
    todo = []
    if lookahead:
        for s in range(S):
            for c in range(NS):
                ext[s * NS + c, 0:HIST_ROWS, :] = ext[s * NS + c, R:R + HIST_ROWS, :]
        stage(un_ref)

        def hand_over(_):
            cybuf[1 - cur] = finish()

        todo = list(pieces) + [hand_over]
    n_slots = 3 + F // FC + D // FC
    quota = [len(todo) // n_slots + (i < len(todo) % n_slots) for i in range(n_slots)]

    def then_conv(value):
        for _ in range(quota.pop(0)):
            todo.pop(0)(_order_token(value))
        return value

    attn_out = then_conv(jnp.dot(ao_ref[...].reshape(rows, -1), wa_ref[...],
                                 preferred_element_type=F32))
    conv_out = then_conv(jnp.dot(cy, wc_ref[...], preferred_element_type=F32))
    merged = (gc_ref[...].reshape(rows, D).astype(F32) * conv_out
              + ga_ref[...].reshape(rows, D).astype(F32) * attn_out)
    upd = then_conv(jnp.dot(merged.astype(BF16), wo_ref[...], preferred_element_type=F32))
    gt1 = mod_ref[:, :, 2 * D:3 * D]
    x1 = x_ref[...] + gt1 * upd.reshape(S, R, D)

    ms = jnp.mean(x1 * x1, axis=-1, keepdims=True)
    xn = x1 * lax.rsqrt(ms + NORM_EPS) * g2_ref[...]
    sh2 = mod_ref[:, :, 3 * D:4 * D]
    sc2 = mod_ref[:, :, 4 * D:5 * D]
    h2 = (xn * (1.0 + sc2) + sh2).reshape(rows, D).astype(BF16)

    for c in range(F // FC):
        gate = then_conv(jnp.dot(h2, wfi_ref[:, c * FC:(c + 1) * FC], preferred_element_type=F32))
        up = jnp.dot(h2, wfi_ref[:, F + c * FC:F + (c + 1) * FC], preferred_element_type=F32)
        act[:, c * FC:(c + 1) * FC] = (_silu(gate) * up).astype(BF16)

    gt2 = mod_ref[:, :, 5 * D:6 * D]
    for c in range(D // FC):
        cols = slice(c * FC, (c + 1) * FC)
        ffn = then_conv(jnp.dot(act[...], wfo_ref[:, cols], preferred_element_type=F32))
        y_ref[:, :, cols] = x1[:, :, cols] + gt2[:, :, cols] * ffn.reshape(S, R, FC)
    assert not todo


def _out_ffn(x, mod, u, hist, w_dw, b_dw, ln_g, ln_b, ao, gc, ga, g2, wc, wa, wo, wfi, wfo, *, S, R):
    nseq, T, D = x.shape
    F = wfo.shape[0]
    FC = 256
    CD = u.shape[2]
    NS = CD // LANES
    n_t = T // R
    assert R % CONV_BLOCK_ROWS == 0 and R >= HIST_ROWS
    w_slabs = jnp.pad(w_dw, ((0, -CONV_K % SUBLANES), (0, 0))).reshape(-1, NS, LANES).swapaxes(0, 1)
    tile = lambda c: pl.BlockSpec((S, R, c), lambda i, t: (i, t, 0))
    return pl.pallas_call(
        functools.partial(_outffn_kernel, S=S, R=R, D=D, F=F, FC=FC, lookahead=n_t > 1),
        grid=(nseq // S, n_t),
        in_specs=[tile(D),
                  pl.BlockSpec((S, 1, mod.shape[2]), lambda i, t: (i, 0, 0)),
                  pl.BlockSpec((S, R, CD), lambda i, t: (i, 0, 0)),
                  pl.BlockSpec((S, R, CD), lambda i, t: (i, jnp.minimum(t + 1, n_t - 1), 0)),
                  pl.BlockSpec((S, CONV_K - 1, CD), lambda i, t: (i, 0, 0)),
                  _resident(w_slabs.shape), _resident(b_dw.shape),
                  _resident(ln_g.shape), _resident(ln_b.shape),
                  tile(ao.shape[2]), tile(D), tile(D),
                  _resident(g2.shape), _resident(wc.shape), _resident(wa.shape),
                  _resident(wo.shape), _resident(wfi.shape), _resident(wfo.shape)],
        out_specs=tile(D),
        out_shape=jax.ShapeDtypeStruct((nseq, T, D), F32),
        scratch_shapes=[pltpu.VMEM((S * NS, R + HIST_ROWS, LANES), F32),
                        pltpu.VMEM((S * NS, R, LANES), F32),
                        pltpu.VMEM((2, S * R, CD), BF16),
                        pltpu.VMEM((S * R, F), BF16)],
        compiler_params=_cparams(2),
        name="out_ffn",
    )(x, mod, u, u, hist, w_slabs, b_dw, ln_g, ln_b, ao, gc, ga, g2, wc, wa, wo, wfi, wfo)


def _tiling(nseq, T, rows):
    R = min(T, rows)
    S = max(1, min(nseq, rows // R))
    assert T % R == 0 and nseq % S == 0
    return S, R


def _layer(x, mod, conv_hist, hk, hv, bias, p, *, TQ):
    nseq, T, D = x.shape
    Si, Ri = _tiling(nseq, T, IN_PROJ_ROWS)
    u, q, kf, vf, kb, vb, gc, ga = _in_proj(
        x, mod, p["g1"], p["w_in"], p["qg"], p["kg"], p["gm"], S=Si, R=Ri)
    S, R = _tiling(nseq, T, OUT_FFN_ROWS)
    Sa, _ = _tiling(nseq, T, max(TQ, ATTN_ROWS))
    ao = _attention(q, kb, vb, hk, hv, bias, S=Sa, TQ=TQ)
    y = _out_ffn(x, mod, u, conv_hist, p["w_dw"], p["b_dw"], p["ln_g"], p["ln_b"], ao, gc, ga,
                 p["g2"], p["w_conv_out"], p["w_attn_out"], p["w_o"], p["w_ffn_in"], p["w_ffn_out"],
                 S=S, R=R)
    return y, u, kf, vf


def kernel(x_prompt, x_sample, c_prompt, c_sample, cache_conv, cache_k, cache_v, norm1_g, norm2_g, w_ada, b_ada, w_in, w_dw, b_dw, conv_ln_g, conv_ln_b, w_conv_out, q_norm_g, k_norm_g, rel_bias, w_attn_out, w_o, w_ffn_in, w_ffn_out):
    depth = norm1_g.shape[0]
    B, T, D = x_prompt.shape
    BS, TS, _ = x_sample.shape
    H, Dh = cache_k.shape[3], cache_k.shape[4]
    AD = H * Dh
    CD = w_dw.shape[2]
    cache_len = cache_k.shape[2]
    assert cache_len == BAND_PAST and TS == CHUNK and T % CHUNK == 0 and Dh == HEAD_DIM
    state_len = min(BAND_PAST, T)
    TQP = 256
    n_c = B + BS
    n_pad = -n_c % 8

    hid = jnp.arange(MXU_DIM) // Dh
    gm = jnp.where(hid[:, None] == hid[None, :], 1.0 / Dh, 0.0).astype(BF16)

    xp, xs = x_prompt, x_sample
    outs = [[] for _ in range(6)]
    for l in range(depth):
        p = {
            "g1": norm1_g[l].reshape(1, D), "g2": norm2_g[l].reshape(1, D),
            "w_in": w_in[l].astype(BF16),
            "qg": jnp.tile(q_norm_g[l], H).reshape(1, AD), "kg": jnp.tile(k_norm_g[l], H).reshape(1, AD),
            "gm": gm,
            "w_dw": w_dw[l], "b_dw": b_dw[l].reshape(1, CD),
            "ln_g": conv_ln_g[l].reshape(1, CD), "ln_b": conv_ln_b[l].reshape(1, CD),
            "w_conv_out": w_conv_out[l].astype(BF16), "w_attn_out": w_attn_out[l].astype(BF16),
            "w_o": w_o[l].astype(BF16), "w_ffn_in": w_ffn_in[l].astype(BF16),
            "w_ffn_out": w_ffn_out[l].astype(BF16),
        }
        c_all = jnp.pad(jnp.concatenate([c_prompt, c_sample], axis=0), ((0, n_pad), (0, 0)))
        mod = _ada(c_all, w_ada[l], b_ada[l])
        mod_p = mod[:B].reshape(B, 1, -1)
        mod_s = mod[B:n_c].reshape(BS, 1, -1)

        bias_p = _bias_tiles(rel_bias[l], TQ=TQP, n_var=BAND_PAST // TQP + 1, first_start=0,
                             NB=BAND_PAST // TQP + 1)
        bias_s = _bias_tiles(rel_bias[l], TQ=TS, n_var=1, first_start=BAND_PAST, NB=1)

        zeros_c = jnp.zeros((B, CONV_K - 1, CD), F32)
        xp, u_p, kf_p, vf_p = _layer(xp, mod_p, zeros_c, None, None, bias_p, p, TQ=TQP)
        hkt = jnp.transpose(cache_k[l], (0, 2, 3, 1)).reshape(BS, AD, cache_len)
        hvt = jnp.transpose(cache_v[l], (0, 2, 3, 1)).reshape(BS, AD, cache_len)
        xs, u_s, kf_s, vf_s = _layer(xs, mod_s, cache_conv[l], hkt, hvt, bias_s, p, TQ=TS)

        outs[0].append(u_p[:, T - (CONV_K - 1):])
        assert state_len <= kf_p.shape[1]
        outs[1].append(kf_p[:, kf_p.shape[1] - state_len:].reshape(B, state_len, H, Dh))
        outs[2].append(vf_p[:, vf_p.shape[1] - state_len:].reshape(B, state_len, H, Dh))
        outs[3].append(u_s[:, TS - (CONV_K - 1):])
        outs[4].append(kf_s.reshape(BS, TS, H, Dh))
        outs[5].append(vf_s.reshape(BS, TS, H, Dh))

    return (xp, xs) + tuple(jnp.stack(o) for o in outs)
```

```python
import functools
import math

import jax
import jax.numpy as jnp
from jax import lax
from jax.experimental import pallas as pl
from jax.experimental.pallas import tpu as pltpu

F32 = jnp.float32
BF16 = jnp.bfloat16

CHUNK = 64
N_PREV_CHUNKS = 8
BAND_PAST = N_PREV_CHUNKS * CHUNK
HEAD_DIM = 64
CONV_K = 31
MAX_REL = 128
NORM_EPS = 1e-6
NEG_INF = -1e30
LOG2_E = math.log2(math.e)
SCORE_SCALE = LOG2_E / math.sqrt(HEAD_DIM)
LANES = 128
SUBLANES = 8
MXU_DIM = 256
CONV_ROW_STRIDE = 4
HIST_ROWS = 32
ROLL_W = 1024
VMEM_LIMIT = 56 * 1024 * 1024
IN_PROJ_ROWS = 1024
OUT_FFN_ROWS = 512
ATTN_ROWS = 256


def _cparams(n_axes):
    return pltpu.CompilerParams(
        dimension_semantics=("arbitrary",) * n_axes, vmem_limit_bytes=VMEM_LIMIT)


def _resident(shape):
    nd = len(shape)
    return pl.BlockSpec(shape, lambda *_: (0,) * nd, pipeline_mode=pl.Buffered(1))


def _sigmoid(x):
    return 1.0 / (1.0 + jnp.exp(-x))


def _silu(x):
    return x * _sigmoid(x)


def _ada_kernel(c_ref, w_ref, b_ref, o_ref):
    a = _silu(c_ref[...]).astype(BF16)
    o_ref[...] = jnp.dot(a, w_ref[...].astype(BF16), preferred_element_type=F32) + b_ref[...]


def _ada(c_all, w_ada, b_ada):
    n, d = c_all.shape
    n_out = w_ada.shape[1]
    tn = 1024
    return pl.pallas_call(
        _ada_kernel,
        grid=(n_out // tn,),
        in_specs=[pl.BlockSpec((n, d), lambda j: (0, 0)),
                  pl.BlockSpec((d, tn), lambda j: (0, j)),
                  pl.BlockSpec((1, tn), lambda j: (0, j))],
        out_specs=pl.BlockSpec((n, tn), lambda j: (0, j)),
        out_shape=jax.ShapeDtypeStruct((n, n_out), F32),
        compiler_params=_cparams(1),
        name="ada",
    )(c_all, w_ada, b_ada.reshape(1, n_out))


CONV_LEAD = HIST_ROWS - (CONV_K - 1)
CONV_BLOCK_ROWS = SUBLANES * CONV_ROW_STRIDE
CONV_GROUP = 4


def _order_token(x):
    bits = pltpu.bitcast(x[:SUBLANES, :LANES].astype(F32), jnp.uint32)
    return lax.shift_right_logical(lax.shift_right_logical(bits, jnp.uint32(16)), jnp.uint32(16))


def _ordered_after(x, token):
    return pltpu.bitcast(pltpu.bitcast(x, jnp.uint32) | token, x.dtype)


def _conv_pieces(ext, ybuf, wdw_ref, *, S, R, NS):
    ST = CONV_ROW_STRIDE
    RB = CONV_BLOCK_ROWS
    G = min(CONV_GROUP, R // RB)

    def piece(slab, c, g0, after=None):
        accs = [None] * (G * ST)
        for j in range(CONV_K):
            tap = jnp.broadcast_to(wdw_ref[c, j:j + 1, :], (SUBLANES, LANES))
            if after is not None:
                tap = _ordered_after(tap, after)
            for b in range(G):
                for q in range(ST):
                    start = g0 + b * RB + CONV_LEAD + q + j
                    term = ext[slab, pl.ds(start, SUBLANES, stride=ST), :] * tap
                    a = b * ST + q
                    accs[a] = term if accs[a] is None else accs[a] + term
        for b in range(G):
            for q in range(ST):
                ybuf[slab, pl.ds(g0 + b * RB + q, SUBLANES, stride=ST), :] = accs[b * ST + q]

    return [functools.partial(piece, s * NS + c, c, g0)
            for s in range(S) for c in range(NS) for g0 in range(0, R, RB * G)]


def _conv_finish(ybuf, bdw_ref, lg_ref, lb_ref, *, S, NS):
    y = jnp.concatenate(
        [jnp.concatenate([ybuf[s * NS + c] for c in range(NS)], axis=-1) for s in range(S)], axis=0)
    y = y + bdw_ref[...]
    mu = jnp.mean(y, axis=-1, keepdims=True)
    yc = y - mu
    var = jnp.mean(yc * yc, axis=-1, keepdims=True)
    yn = yc * lax.rsqrt(var + NORM_EPS) * lg_ref[...] + lb_ref[...]
    return _silu(yn).astype(BF16)


def _inproj_kernel(x_ref, mod_ref, g1_ref, w_ref, qg_ref, kg_ref, gm_ref,
                   u_ref, q_ref, kf_ref, vf_ref, kb_ref, vb_ref, gc_ref, ga_ref,
                   *, S, R, D, CD, AD):
    rows = S * R

    x = x_ref[...]
    ms = jnp.mean(x * x, axis=-1, keepdims=True)
    xn = x * lax.rsqrt(ms + NORM_EPS) * g1_ref[...]
    sh = mod_ref[:, :, 0:D]
    sc = mod_ref[:, :, D:2 * D]
    h = (xn * (1.0 + sc) + sh).reshape(rows, D).astype(BF16)

    def sec(lo, width):
        return jnp.dot(h, w_ref[:, lo:lo + width], preferred_element_type=F32)

    def head_rms(z, g_ref):
        zz = (z * z).astype(BF16)
        gw = gm_ref.shape[0]
        msq = jnp.concatenate(
            [jnp.dot(zz[:, c:c + gw], gm_ref[...], preferred_element_type=F32)
             for c in range(0, AD, gw)], axis=-1)
        return z * lax.rsqrt(msq + NORM_EPS) * g_ref[...]

    u = sec(0, CD) * _sigmoid(sec(CD, CD))
    u_ref[...] = u.reshape(S, R, CD)

    o = 2 * CD
    qn = head_rms(sec(o, AD), qg_ref) * SCORE_SCALE
    q_ref[...] = qn.astype(BF16).reshape(S, R, AD)
    kn = head_rms(sec(o + AD, AD), kg_ref)
    kf_ref[...] = kn.reshape(S, R, AD)
    kb_ref[...] = kn.astype(BF16).reshape(S, R, AD)
    v = sec(o + 2 * AD, AD)
    vf_ref[...] = v.reshape(S, R, AD)
    vb_ref[...] = v.astype(BF16).reshape(S, R, AD)
    o = o + 3 * AD
    gc_ref[...] = _sigmoid(sec(o, D)).astype(BF16).reshape(S, R, D)
    ga_ref[...] = _sigmoid(sec(o + D, D)).astype(BF16).reshape(S, R, D)


def _in_proj(x, mod, g1, w_in, qg, kg, gm, *, S, R):
    nseq, T, D = x.shape
    AD = qg.shape[1]
    CD = (w_in.shape[1] - 3 * AD - 2 * D) // 2
    grid = (nseq // S, T // R)
    tile = lambda c: pl.BlockSpec((S, R, c), lambda i, t: (i, t, 0))
    last = lambda c: pl.BlockSpec((S, R, c), lambda i, t: (i, 0, 0))
    outs = [(CD, F32, tile, T), (AD, BF16, tile, T), (AD, F32, last, R), (AD, F32, last, R),
            (AD, BF16, tile, T), (AD, BF16, tile, T), (D, BF16, tile, T), (D, BF16, tile, T)]
    return pl.pallas_call(
        functools.partial(_inproj_kernel, S=S, R=R, D=D, CD=CD, AD=AD),
        grid=grid,
        in_specs=[tile(D),
                  pl.BlockSpec((S, 1, mod.shape[2]), lambda i, t: (i, 0, 0)),
                  _resident(g1.shape), _resident(w_in.shape), _resident(qg.shape),
                  _resident(kg.shape), _resident(gm.shape)],
        out_specs=[spec(c) for c, _, spec, _ in outs],
        out_shape=[jax.ShapeDtypeStruct((nseq, n, c), dt) for c, dt, _, n in outs],
        compiler_params=_cparams(2),
        name="in_proj",
    )(x, mod, g1, w_in, qg, kg, gm)


def _bias_kernel(e_ref, o_ref, *, TQ, TK, NB, first_start):
    start = first_start + pl.program_id(0) * TQ
    qi = lax.broadcasted_iota(jnp.int32, (TQ, TK), 0)
    ki = lax.broadcasted_iota(jnp.int32, (TQ, TK), 1)
    qc = qi // CHUNK
    kc = ki // CHUNK
    valid = (kc >= qc) & (kc <= qc + N_PREV_CHUNKS) & (ki + start >= BAND_PAST)
    BK = TK // NB
    for h in range(e_ref.shape[0]):
        row = jnp.broadcast_to(e_ref[h], (TQ, ROLL_W))
        b = pltpu.roll(row, 0, 1, stride=1, stride_axis=0)[:, :TK]
        tile = jnp.where(valid, b * LOG2_E, NEG_INF)
        for j in range(NB):
            o_ref[0, h // 2, j, (h % 2) * TQ:(h % 2 + 1) * TQ, :] = tile[:, j * BK:(j + 1) * BK]


def _bias_tiles(rel_bias, *, TQ, n_var, first_start, NB):
    H = rel_bias.shape[0]
    TK = BAND_PAST + TQ
    assert TQ + TK <= ROLL_W
    far = rel_bias[:, 2 * MAX_REL:]
    near = rel_bias[:, :1]
    by_m = jnp.concatenate([
        jnp.broadcast_to(far, (H, BAND_PAST - MAX_REL)),
        rel_bias[:, ::-1],
        jnp.broadcast_to(near, (H, ROLL_W - (BAND_PAST + MAX_REL) - 1)),
    ], axis=1)
    e = jnp.concatenate([by_m[:, :TK], jnp.broadcast_to(far, (H, ROLL_W - TK))], axis=1)
    e = e.reshape(H, 1, ROLL_W)
    return pl.pallas_call(
        functools.partial(_bias_kernel, TQ=TQ, TK=TK, NB=NB, first_start=first_start),
        grid=(n_var,),
        in_specs=[pl.BlockSpec((H, 1, ROLL_W), lambda j: (0, 0, 0))],
        out_specs=pl.BlockSpec((1, H // 2, NB, 2 * TQ, TK // NB), lambda j: (j, 0, 0, 0, 0)),
        out_shape=jax.ShapeDtypeStruct((n_var, H // 2, NB, 2 * TQ, TK // NB), F32),
        compiler_params=_cparams(1),
        name="bias_tiles",
    )(e)


_NT = (((1,), (1,)), ((), ()))
SCORE_LOOKAHEAD = 2


def _lane_cols(p):
    return slice(p * LANES, (p + 1) * LANES)


def _first_head_lanes():
    return lax.broadcasted_iota(jnp.int32, (1, LANES), 1) < HEAD_DIM


def _stack_pair_queries(q2, low):
    zero = jnp.zeros_like(q2)
    return jnp.concatenate([jnp.where(low, q2, zero), jnp.where(low, zero, q2)], axis=0)


def _attend_units(units, scores, weighted_values, o_ref, TQ, low):
    ahead = [scores(*u) for u in units[:SCORE_LOOKAHEAD]]
    for n, (s, p) in enumerate(units):
        sc = ahead.pop(0)
        if n + SCORE_LOOKAHEAD < len(units):
            ahead.append(scores(*units[n + SCORE_LOOKAHEAD]))
        mx = jnp.max(sc, axis=-1, keepdims=True)
        pr = jnp.exp2(sc - mx)
        den = jnp.sum(pr, axis=-1, keepdims=True)
        oo = weighted_values(s, p, pr.astype(BF16)) / den
        o_ref[s, :, _lane_cols(p)] = jnp.where(low, oo[:TQ], oo[TQ:]).astype(BF16)


def _attn_ring_kernel(q_ref, k_ref, v_ref, bias_ref, o_ref, kbuf, vbuf, *, S, TQ, H):
    t = pl.program_id(1)
    NS = kbuf.shape[1] // TQ
    cur = (t + NS - 1) % NS

    @pl.when(t == 0)
    def _():
        kbuf[...] = jnp.zeros_like(kbuf)
        vbuf[...] = jnp.zeros_like(vbuf)

    row0 = pl.multiple_of(cur * TQ, TQ)
    for s in range(S):
        kbuf[s, pl.ds(row0, TQ), :] = k_ref[s]
        vbuf[s, pl.ds(row0, TQ), :] = v_ref[s]

    low = _first_head_lanes()
    logical = [(j + NS - 1 - cur) % NS for j in range(NS)]

    def scores(s, p):
        qq = _stack_pair_queries(q_ref[s, :, _lane_cols(p)], low)
        sc = lax.dot_general(qq, kbuf[s, :, _lane_cols(p)], _NT, preferred_element_type=F32)
        return sc + jnp.concatenate([bias_ref[0, p, logical[j]] for j in range(NS)], axis=-1)

    def weighted_values(s, p, pr):
        return jnp.dot(pr, vbuf[s, :, _lane_cols(p)], preferred_element_type=F32)

    units = [(s, p) for s in range(S) for p in range(H // 2)]
    _attend_units(units, scores, weighted_values, o_ref, TQ, low)


def _attn_cached_kernel(q_ref, k_ref, v_ref, hkt_ref, hvt_ref, bias_ref, o_ref, *, S, TQ, H):
    low = _first_head_lanes()

    def scores(s, p):
        qq = _stack_pair_queries(q_ref[s, :, _lane_cols(p)], low)
        past = jnp.dot(qq, hkt_ref[s, _lane_cols(p), :].astype(BF16), preferred_element_type=F32)
        own = lax.dot_general(qq, k_ref[s, :, _lane_cols(p)], _NT, preferred_element_type=F32)
        return jnp.concatenate([past, own], axis=-1) + bias_ref[0, p, 0]

    def weighted_values(s, p, pr):
        past = lax.dot_general(pr[:, :BAND_PAST], hvt_ref[s, _lane_cols(p), :].astype(BF16), _NT,
                               preferred_element_type=F32)
        own = jnp.dot(pr[:, BAND_PAST:], v_ref[s, :, _lane_cols(p)], preferred_element_type=F32)
        return past + own

    units = [(s, p) for s in range(S) for p in range(H // 2)]
    _attend_units(units, scores, weighted_values, o_ref, TQ, low)


def _attention(q, kb, vb, hkt, hvt, bias, *, S, TQ):
    nseq, T, AD = q.shape
    H = AD // HEAD_DIM
    TK = BAND_PAST + TQ
    n_var, HP, NB, TQ2, BK = bias.shape
    assert HP * 2 == H and TQ2 == 2 * TQ and NB * BK == TK
    tile = pl.BlockSpec((S, TQ, AD), lambda i, t: (i, t, 0))
    bias_spec = pl.BlockSpec((1, HP, NB, TQ2, BK),
                             lambda i, t: (jnp.minimum(t, n_var - 1), 0, 0, 0, 0))
    out_shape = jax.ShapeDtypeStruct((nseq, T, AD), BF16)
    if hkt is None:
        assert BK == TQ and TQ % LANES == 0
        return pl.pallas_call(
            functools.partial(_attn_ring_kernel, S=S, TQ=TQ, H=H),
            grid=(nseq // S, T // TQ),
            in_specs=[tile, tile, tile, bias_spec],
            out_specs=tile,
            out_shape=out_shape,
            scratch_shapes=[pltpu.VMEM((S, TK, AD), BF16), pltpu.VMEM((S, TK, AD), BF16)],
            compiler_params=_cparams(2),
            name="attn_ring",
        )(q, kb, vb, bias)
    assert T == TQ and NB == 1 and n_var == 1
    hist = pl.BlockSpec((S, AD, BAND_PAST), lambda i, t: (i, 0, 0))
    return pl.pallas_call(
        functools.partial(_attn_cached_kernel, S=S, TQ=TQ, H=H),
        grid=(nseq // S, 1),
        in_specs=[tile, tile, tile, hist, hist, bias_spec],
        out_specs=tile,
        out_shape=out_shape,
        compiler_params=_cparams(2),
        name="attn_cached",
    )(q, kb, vb, hkt, hvt, bias)


def _outffn_kernel(x_ref, mod_ref, u_ref, hist_ref, un_ref, histn_ref, wdw_ref, bdw_ref, lg_ref,
                   lb_ref, ao_ref, gc_ref, ga_ref, g2_ref,
                   wc_ref, wa_ref, wo_ref, wfi_ref, wfo_ref, y_ref, ext, ybuf, cybuf, act,
                   *, S, R, D, F, FC):
    rows = S * R
    NS = u_ref.shape[2] // LANES
    t = pl.program_id(1)
    step = pl.program_id(0) * pl.num_programs(1) + t
    pieces = _conv_pieces(ext, ybuf, wdw_ref, S=S, R=R, NS=NS)
    finish = functools.partial(_conv_finish, ybuf, bdw_ref, lg_ref, lb_ref, S=S, NS=NS)

    def stage(tile_ref):
        for s in range(S):
            for c in range(NS):
                ext[s * NS + c, HIST_ROWS:HIST_ROWS + R, :] = tile_ref[s, :, c * LANES:(c + 1) * LANES]

    def restart(h_ref):
        for s in range(S):
            for c in range(NS):
                ext[s * NS + c, CONV_LEAD:HIST_ROWS, :] = h_ref[s, :, c * LANES:(c + 1) * LANES]

    @pl.when(step == 0)
    def _():
        restart(hist_ref)
        stage(u_ref)
        for piece in pieces:
            piece()
        cybuf[0] = finish()

    @pl.when(t + 1 < pl.num_programs(1))
    def _():
        for s in range(S):
            for c in range(NS):
                ext[s * NS + c, 0:HIST_ROWS, :] = ext[s * NS + c, R:R + HIST_ROWS, :]

    @pl.when(t + 1 == pl.num_programs(1))
    def _():
        restart(histn_ref)

    cur = step % 2
    cy = cybuf[cur]
    stage(un_ref)

    def hand_over(_):
        cybuf[1 - cur] = finish()

    todo = list(pieces) + [hand_over]
    n_slots = 3 + F // FC + D // FC
    quota = [len(todo) // n_slots + (i < len(todo) % n_slots) for i in range(n_slots)]

    def then_conv(value):
        for _ in range(quota.pop(0)):
            todo.pop(0)(_order_token(value))
        return value

    attn_out = then_conv(jnp.dot(ao_ref[...].reshape(rows, -1), wa_ref[...],
                                 preferred_element_type=F32))
    conv_out = then_conv(jnp.dot(cy, wc_ref[...], preferred_element_type=F32))
    merged = (gc_ref[...].reshape(rows, D).astype(F32) * conv_out
              + ga_ref[...].reshape(rows, D).astype(F32) * attn_out)
    upd = then_conv(jnp.dot(merged.astype(BF16), wo_ref[...], preferred_element_type=F32))
    gt1 = mod_ref[:, :, 2 * D:3 * D]
    x1 = x_ref[...] + gt1 * upd.reshape(S, R, D)

    ms = jnp.mean(x1 * x1, axis=-1, keepdims=True)
    xn = x1 * lax.rsqrt(ms + NORM_EPS) * g2_ref[...]
    sh2 = mod_ref[:, :, 3 * D:4 * D]
    sc2 = mod_ref[:, :, 4 * D:5 * D]
    h2 = (xn * (1.0 + sc2) + sh2).reshape(rows, D).astype(BF16)

    for c in range(F // FC):
        gate = then_conv(jnp.dot(h2, wfi_ref[:, c * FC:(c + 1) * FC], preferred_element_type=F32))
        up = jnp.dot(h2, wfi_ref[:, F + c * FC:F + (c + 1) * FC], preferred_element_type=F32)
        act[:, c * FC:(c + 1) * FC] = (_silu(gate) * up).astype(BF16)

    gt2 = mod_ref[:, :, 5 * D:6 * D]
    for c in range(D // FC):
        cols = slice(c * FC, (c + 1) * FC)
        ffn = then_conv(jnp.dot(act[...], wfo_ref[:, cols], preferred_element_type=F32))
        y_ref[:, :, cols] = x1[:, :, cols] + gt2[:, :, cols] * ffn.reshape(S, R, FC)
    assert not todo


def _out_ffn(x, mod, u, hist, w_dw, b_dw, ln_g, ln_b, ao, gc, ga, g2, wc, wa, wo, wfi, wfo, *, S, R):
    nseq, T, D = x.shape
    F = wfo.shape[0]
    FC = 256
    CD = u.shape[2]
    NS = CD // LANES
    n_t = T // R
    assert R % CONV_BLOCK_ROWS == 0 and R >= HIST_ROWS
    w_slabs = jnp.pad(w_dw, ((0, -CONV_K % SUBLANES), (0, 0))).reshape(-1, NS, LANES).swapaxes(0, 1)
    tile = lambda c: pl.BlockSpec((S, R, c), lambda i, t: (i, t, 0))
    n_g = nseq // S

    def next_step(i, t):
        wrap = (t + 1) // n_t
        return jnp.minimum(i + wrap, n_g - 1), (t + 1) % n_t

    return pl.pallas_call(
        functools.partial(_outffn_kernel, S=S, R=R, D=D, F=F, FC=FC),
        grid=(n_g, n_t),
        in_specs=[tile(D),
                  pl.BlockSpec((S, 1, mod.shape[2]), lambda i, t: (i, 0, 0)),
                  pl.BlockSpec((S, R, CD), lambda i, t: (0, 0, 0)),
                  pl.BlockSpec((S, CONV_K - 1, CD), lambda i, t: (0, 0, 0)),
                  pl.BlockSpec((S, R, CD), lambda i, t: (*next_step(i, t), 0)),
                  pl.BlockSpec((S, CONV_K - 1, CD), lambda i, t: (next_step(i, t)[0], 0, 0)),
                  _resident(w_slabs.shape), _resident(b_dw.shape),
                  _resident(ln_g.shape), _resident(ln_b.shape),
                  tile(ao.shape[2]), tile(D), tile(D),
                  _resident(g2.shape), _resident(wc.shape), _resident(wa.shape),
                  _resident(wo.shape), _resident(wfi.shape), _resident(wfo.shape)],
        out_specs=tile(D),
        out_shape=jax.ShapeDtypeStruct((nseq, T, D), F32),
        scratch_shapes=[pltpu.VMEM((S * NS, R + HIST_ROWS, LANES), F32),
                        pltpu.VMEM((S * NS, R, LANES), F32),
                        pltpu.VMEM((2, S * R, CD), BF16),
                        pltpu.VMEM((S * R, F), BF16)],
        compiler_params=_cparams(2),
        name="out_ffn",
    )(x, mod, u, hist, u, hist, w_slabs, b_dw, ln_g, ln_b, ao, gc, ga, g2, wc, wa, wo, wfi, wfo)


def _tiling(nseq, T, rows):
    R = min(T, rows)
    S = max(1, min(nseq, rows // R))
    assert T % R == 0 and nseq % S == 0
    return S, R


def _layer(x, mod, conv_hist, hk, hv, bias, p, *, TQ):
    nseq, T, D = x.shape
    Si, Ri = _tiling(nseq, T, IN_PROJ_ROWS)
    u, q, kf, vf, kb, vb, gc, ga = _in_proj(
        x, mod, p["g1"], p["w_in"], p["qg"], p["kg"], p["gm"], S=Si, R=Ri)
    S, R = _tiling(nseq, T, OUT_FFN_ROWS)
    Sa, _ = _tiling(nseq, T, max(TQ, ATTN_ROWS))
    ao = _attention(q, kb, vb, hk, hv, bias, S=Sa, TQ=TQ)
    y = _out_ffn(x, mod, u, conv_hist, p["w_dw"], p["b_dw"], p["ln_g"], p["ln_b"], ao, gc, ga,
                 p["g2"], p["w_conv_out"], p["w_attn_out"], p["w_o"], p["w_ffn_in"], p["w_ffn_out"],
                 S=S, R=R)
    return y, u, kf, vf


def kernel(x_prompt, x_sample, c_prompt, c_sample, cache_conv, cache_k, cache_v, norm1_g, norm2_g, w_ada, b_ada, w_in, w_dw, b_dw, conv_ln_g, conv_ln_b, w_conv_out, q_norm_g, k_norm_g, rel_bias, w_attn_out, w_o, w_ffn_in, w_ffn_out):
    depth = norm1_g.shape[0]
    B, T, D = x_prompt.shape
    BS, TS, _ = x_sample.shape
    H, Dh = cache_k.shape[3], cache_k.shape[4]
    AD = H * Dh
    CD = w_dw.shape[2]
    cache_len = cache_k.shape[2]
    assert cache_len == BAND_PAST and TS == CHUNK and T % CHUNK == 0 and Dh == HEAD_DIM
    state_len = min(BAND_PAST, T)
    TQP = 256
    n_c = B + BS
    n_pad = -n_c % 8

    hid = jnp.arange(MXU_DIM) // Dh
    gm = jnp.where(hid[:, None] == hid[None, :], 1.0 / Dh, 0.0).astype(BF16)

    xp, xs = x_prompt, x_sample
    outs = [[] for _ in range(6)]
    for l in range(depth):
        p = {
            "g1": norm1_g[l].reshape(1, D), "g2": norm2_g[l].reshape(1, D),
            "w_in": w_in[l].astype(BF16),
            "qg": jnp.tile(q_norm_g[l], H).reshape(1, AD), "kg": jnp.tile(k_norm_g[l], H).reshape(1, AD),
            "gm": gm,
            "w_dw": w_dw[l], "b_dw": b_dw[l].reshape(1, CD),
            "ln_g": conv_ln_g[l].reshape(1, CD), "ln_b": conv_ln_b[l].reshape(1, CD),
            "w_conv_out": w_conv_out[l].astype(BF16), "w_attn_out": w_attn_out[l].astype(BF16),
            "w_o": w_o[l].astype(BF16), "w_ffn_in": w_ffn_in[l].astype(BF16),
            "w_ffn_out": w_ffn_out[l].astype(BF16),
        }
        c_all = jnp.pad(jnp.concatenate([c_prompt, c_sample], axis=0), ((0, n_pad), (0, 0)))
        mod = _ada(c_all, w_ada[l], b_ada[l])
        mod_p = mod[:B].reshape(B, 1, -1)
        mod_s = mod[B:n_c].reshape(BS, 1, -1)

        bias_p = _bias_tiles(rel_bias[l], TQ=TQP, n_var=BAND_PAST // TQP + 1, first_start=0,
                             NB=BAND_PAST // TQP + 1)
        bias_s = _bias_tiles(rel_bias[l], TQ=TS, n_var=1, first_start=BAND_PAST, NB=1)

        zeros_c = jnp.zeros((B, CONV_K - 1, CD), F32)
        xp, u_p, kf_p, vf_p = _layer(xp, mod_p, zeros_c, None, None, bias_p, p, TQ=TQP)
        hkt = jnp.transpose(cache_k[l], (0, 2, 3, 1)).reshape(BS, AD, cache_len)
        hvt = jnp.transpose(cache_v[l], (0, 2, 3, 1)).reshape(BS, AD, cache_len)
        xs, u_s, kf_s, vf_s = _layer(xs, mod_s, cache_conv[l], hkt, hvt, bias_s, p, TQ=TS)

        outs[0].append(u_p[:, T - (CONV_K - 1):])
        assert state_len <= kf_p.shape[1]
        outs[1].append(kf_p[:, kf_p.shape[1] - state_len:].reshape(B, state_len, H, Dh))
        outs[2].append(vf_p[:, vf_p.shape[1] - state_len:].reshape(B, state_len, H, Dh))
        outs[3].append(u_s[:, TS - (CONV_K - 1):])
        outs[4].append(kf_s.reshape(BS, TS, H, Dh))
        outs[5].append(vf_s.reshape(BS, TS, H, Dh))

    return (xp, xs) + tuple(jnp.stack(o) for o in outs)
```

```python
import functools
import math

import jax
import jax.numpy as jnp
from jax import lax
from jax.experimental import pallas as pl
from jax.experimental.pallas import tpu as pltpu

F32 = jnp.float32
BF16 = jnp.bfloat16

CHUNK = 64
N_PREV_CHUNKS = 8
BAND_PAST = N_PREV_CHUNKS * CHUNK
HEAD_DIM = 64
CONV_K = 31
MAX_REL = 128
NORM_EPS = 1e-6
NEG_INF = -1e30
LOG2_E = math.log2(math.e)
SCORE_SCALE = LOG2_E / math.sqrt(HEAD_DIM)
LANES = 128
SUBLANES = 8
MXU_DIM = 256
CONV_ROW_STRIDE = 4
HIST_ROWS = 32
ROLL_W = 1024
VMEM_LIMIT = 56 * 1024 * 1024
IN_PROJ_ROWS = 1024
OUT_FFN_ROWS = 512
ATTN_ROWS = 256


def _cparams(n_axes):
    return pltpu.CompilerParams(
        dimension_semantics=("arbitrary",) * n_axes, vmem_limit_bytes=VMEM_LIMIT)


def _resident(shape):
    nd = len(shape)
    return pl.BlockSpec(shape, lambda *_: (0,) * nd, pipeline_mode=pl.Buffered(1))


def _sigmoid(x):
    return 1.0 / (1.0 + jnp.exp(-x))


def _silu(x):
    return x * _sigmoid(x)


def _ada_kernel(c_ref, w_ref, b_ref, o_ref):
    a = _silu(c_ref[...]).astype(BF16)
    o_ref[...] = jnp.dot(a, w_ref[...].astype(BF16), preferred_element_type=F32) + b_ref[...]


def _ada(c_all, w_ada, b_ada):
    n, d = c_all.shape
    n_out = w_ada.shape[1]
    tn = 1024
    return pl.pallas_call(
        _ada_kernel,
        grid=(n_out // tn,),
        in_specs=[pl.BlockSpec((n, d), lambda j: (0, 0)),
                  pl.BlockSpec((d, tn), lambda j: (0, j)),
                  pl.BlockSpec((1, tn), lambda j: (0, j))],
        out_specs=pl.BlockSpec((n, tn), lambda j: (0, j)),
        out_shape=jax.ShapeDtypeStruct((n, n_out), F32),
        compiler_params=_cparams(1),
        name="ada",
    )(c_all, w_ada, b_ada.reshape(1, n_out))


CONV_LEAD = HIST_ROWS - (CONV_K - 1)
CONV_BLOCK_ROWS = SUBLANES * CONV_ROW_STRIDE
CONV_GROUP = 4


def _order_token(x):
    bits = pltpu.bitcast(x[:SUBLANES, :LANES].astype(F32), jnp.uint32)
    return lax.shift_right_logical(lax.shift_right_logical(bits, jnp.uint32(16)), jnp.uint32(16))


def _ordered_after(x, token):
    return pltpu.bitcast(pltpu.bitcast(x, jnp.uint32) | token, x.dtype)


def _conv_pieces(ext, ybuf, wdw_ref, *, S, R, NS):
    ST = CONV_ROW_STRIDE
    RB = CONV_BLOCK_ROWS
    G = min(CONV_GROUP, R // RB)

    def piece(slab, c, g0, after=None):
        accs = [None] * (G * ST)
        for j in range(CONV_K):
            tap = jnp.broadcast_to(wdw_ref[c, j:j + 1, :], (SUBLANES, LANES))
            if after is not None:
                tap = _ordered_after(tap, after)
            for b in range(G):
                for q in range(ST):
                    start = g0 + b * RB + CONV_LEAD + q + j
                    term = ext[slab, pl.ds(start, SUBLANES, stride=ST), :] * tap
                    a = b * ST + q
                    accs[a] = term if accs[a] is None else accs[a] + term
        for b in range(G):
            for q in range(ST):
                ybuf[slab, pl.ds(g0 + b * RB + q, SUBLANES, stride=ST), :] = accs[b * ST + q]

    return [functools.partial(piece, s * NS + c, c, g0)
            for s in range(S) for c in range(NS) for g0 in range(0, R, RB * G)]


def _conv_finish(ybuf, bdw_ref, lg_ref, lb_ref, *, S, NS):
    y = jnp.concatenate(
        [jnp.concatenate([ybuf[s * NS + c] for c in range(NS)], axis=-1) for s in range(S)], axis=0)
    y = y + bdw_ref[...]
    mu = jnp.mean(y, axis=-1, keepdims=True)
    yc = y - mu
    var = jnp.mean(yc * yc, axis=-1, keepdims=True)
    yn = yc * lax.rsqrt(var + NORM_EPS) * lg_ref[...] + lb_ref[...]
    return _silu(yn).astype(BF16)


def _inproj_kernel(x_ref, mod_ref, g1_ref, w_ref, qg_ref, kg_ref, gm_ref,
                   u_ref, q_ref, kf_ref, vf_ref, kb_ref, vb_ref, gc_ref, ga_ref,
                   *, S, R, D, CD, AD):
    rows = S * R

    x = x_ref[...]
    ms = jnp.mean(x * x, axis=-1, keepdims=True)
    xn = x * lax.rsqrt(ms + NORM_EPS) * g1_ref[...]
    sh = mod_ref[:, :, 0:D]
    sc = mod_ref[:, :, D:2 * D]
    h = (xn * (1.0 + sc) + sh).reshape(rows, D).astype(BF16)

    def sec(lo, width):
        return jnp.dot(h, w_ref[:, lo:lo + width], preferred_element_type=F32)

    def head_rms(z, g_ref):
        zz = (z * z).astype(BF16)
        gw = gm_ref.shape[0]
        msq = jnp.concatenate(
            [jnp.dot(zz[:, c:c + gw], gm_ref[...], preferred_element_type=F32)
             for c in range(0, AD, gw)], axis=-1)
        return z * lax.rsqrt(msq + NORM_EPS) * g_ref[...]

    u = sec(0, CD) * _sigmoid(sec(CD, CD))
    u_ref[...] = u.reshape(S, R, CD)

    o = 2 * CD
    qn = head_rms(sec(o, AD), qg_ref) * SCORE_SCALE
    q_ref[...] = qn.astype(BF16).reshape(S, R, AD)
    kn = head_rms(sec(o + AD, AD), kg_ref)
    kf_ref[...] = kn.reshape(S, R, AD)
    kb_ref[...] = kn.astype(BF16).reshape(S, R, AD)
    v = sec(o + 2 * AD, AD)
    vf_ref[...] = v.reshape(S, R, AD)
    vb_ref[...] = v.astype(BF16).reshape(S, R, AD)
    o = o + 3 * AD
    gc_ref[...] = _sigmoid(sec(o, D)).astype(BF16).reshape(S, R, D)
    ga_ref[...] = _sigmoid(sec(o + D, D)).astype(BF16).reshape(S, R, D)


def _in_proj(x, mod, g1, w_in, qg, kg, gm, *, S, R):
    nseq, T, D = x.shape
    AD = qg.shape[1]
    CD = (w_in.shape[1] - 3 * AD - 2 * D) // 2
    grid = (nseq // S, T // R)
    tile = lambda c: pl.BlockSpec((S, R, c), lambda i, t: (i, t, 0))
    last = lambda c: pl.BlockSpec((S, R, c), lambda i, t: (i, 0, 0))
    outs = [(CD, F32, tile, T), (AD, BF16, tile, T), (AD, F32, last, R), (AD, F32, last, R),
            (AD, BF16, tile, T), (AD, BF16, tile, T), (D, BF16, tile, T), (D, BF16, tile, T)]
    return pl.pallas_call(
        functools.partial(_inproj_kernel, S=S, R=R, D=D, CD=CD, AD=AD),
        grid=grid,
        in_specs=[tile(D),
                  pl.BlockSpec((S, 1, mod.shape[2]), lambda i, t: (i, 0, 0)),
                  _resident(g1.shape), _resident(w_in.shape), _resident(qg.shape),
                  _resident(kg.shape), _resident(gm.shape)],
        out_specs=[spec(c) for c, _, spec, _ in outs],
        out_shape=[jax.ShapeDtypeStruct((nseq, n, c), dt) for c, dt, _, n in outs],
        compiler_params=_cparams(2),
        name="in_proj",
    )(x, mod, g1, w_in, qg, kg, gm)


def _bias_kernel(e_ref, o_ref, *, TQ, TK, NB, first_start):
    start = first_start + pl.program_id(0) * TQ
    qi = lax.broadcasted_iota(jnp.int32, (TQ, TK), 0)
    ki = lax.broadcasted_iota(jnp.int32, (TQ, TK), 1)
    qc = qi // CHUNK
    kc = ki // CHUNK
    valid = (kc >= qc) & (kc <= qc + N_PREV_CHUNKS) & (ki + start >= BAND_PAST)
    BK = TK // NB
    for h in range(e_ref.shape[0]):
        row = jnp.broadcast_to(e_ref[h], (TQ, ROLL_W))
        b = pltpu.roll(row, 0, 1, stride=1, stride_axis=0)[:, :TK]
        tile = jnp.where(valid, b * LOG2_E, NEG_INF)
        for j in range(NB):
            o_ref[0, h // 2, j, (h % 2) * TQ:(h % 2 + 1) * TQ, :] = tile[:, j * BK:(j + 1) * BK]


def _bias_tiles(rel_bias, *, TQ, n_var, first_start, NB):
    H = rel_bias.shape[0]
    TK = BAND_PAST + TQ
    assert TQ + TK <= ROLL_W
    far = rel_bias[:, 2 * MAX_REL:]
    near = rel_bias[:, :1]
    by_m = jnp.concatenate([
        jnp.broadcast_to(far, (H, BAND_PAST - MAX_REL)),
        rel_bias[:, ::-1],
        jnp.broadcast_to(near, (H, ROLL_W - (BAND_PAST + MAX_REL) - 1)),
    ], axis=1)
    e = jnp.concatenate([by_m[:, :TK], jnp.broadcast_to(far, (H, ROLL_W - TK))], axis=1)
    e = e.reshape(H, 1, ROLL_W)
    return pl.pallas_call(
        functools.partial(_bias_kernel, TQ=TQ, TK=TK, NB=NB, first_start=first_start),
        grid=(n_var,),
        in_specs=[pl.BlockSpec((H, 1, ROLL_W), lambda j: (0, 0, 0))],
        out_specs=pl.BlockSpec((1, H // 2, NB, 2 * TQ, TK // NB), lambda j: (j, 0, 0, 0, 0)),
        out_shape=jax.ShapeDtypeStruct((n_var, H // 2, NB, 2 * TQ, TK // NB), F32),
        compiler_params=_cparams(1),
        name="bias_tiles",
    )(e)


_NT = (((1,), (1,)), ((), ()))
SCORE_LOOKAHEAD = 2


def _lane_cols(p):
    return slice(p * LANES, (p + 1) * LANES)


def _first_head_lanes():
    return lax.broadcasted_iota(jnp.int32, (1, LANES), 1) < HEAD_DIM


def _stack_pair_queries(q2, low):
    zero = jnp.zeros_like(q2)
    return jnp.concatenate([jnp.where(low, q2, zero), jnp.where(low, zero, q2)], axis=0)


def _attend_units(units, scores, weighted_values, o_ref, TQ, low):
    ahead = [scores(*u) for u in units[:SCORE_LOOKAHEAD]]
    for n, (s, p) in enumerate(units):
        sc = ahead.pop(0)
        if n + SCORE_LOOKAHEAD < len(units):
            ahead.append(scores(*units[n + SCORE_LOOKAHEAD]))
        mx = jnp.max(sc, axis=-1, keepdims=True)
        pr = jnp.exp2(sc - mx)
        den = jnp.sum(pr, axis=-1, keepdims=True)
        oo = weighted_values(s, p, pr.astype(BF16)) / den
        o_ref[s, :, _lane_cols(p)] = jnp.where(low, oo[:TQ], oo[TQ:]).astype(BF16)


def _attn_ring_kernel(q_ref, k_ref, v_ref, bias_ref, o_ref, kbuf, vbuf, *, S, TQ, H):
    t = pl.program_id(1)
    NS = kbuf.shape[1] // TQ
    cur = (t + NS - 1) % NS

    @pl.when(t == 0)
    def _():
        kbuf[...] = jnp.zeros_like(kbuf)
        vbuf[...] = jnp.zeros_like(vbuf)

    row0 = pl.multiple_of(cur * TQ, TQ)
    for s in range(S):
        kbuf[s, pl.ds(row0, TQ), :] = k_ref[s]
        vbuf[s, pl.ds(row0, TQ), :] = v_ref[s]

    low = _first_head_lanes()
    logical = [(j + NS - 1 - cur) % NS for j in range(NS)]

    def scores(s, p):
        qq = _stack_pair_queries(q_ref[s, :, _lane_cols(p)], low)
        sc = lax.dot_general(qq, kbuf[s, :, _lane_cols(p)], _NT, preferred_element_type=F32)
        return sc + jnp.concatenate([bias_ref[0, p, logical[j]] for j in range(NS)], axis=-1)

    def weighted_values(s, p, pr):
        return jnp.dot(pr, vbuf[s, :, _lane_cols(p)], preferred_element_type=F32)

    units = [(s, p) for s in range(S) for p in range(H // 2)]
    _attend_units(units, scores, weighted_values, o_ref, TQ, low)


def _attn_cached_kernel(q_ref, k_ref, v_ref, hkt_ref, hvt_ref, bias_ref, o_ref, *, S, TQ, H):
    low = _first_head_lanes()

    def scores(s, p):
        qq = _stack_pair_queries(q_ref[s, :, _lane_cols(p)], low)
        past = jnp.dot(qq, hkt_ref[s, _lane_cols(p), :].astype(BF16), preferred_element_type=F32)
        own = lax.dot_general(qq, k_ref[s, :, _lane_cols(p)], _NT, preferred_element_type=F32)
        return jnp.concatenate([past, own], axis=-1) + bias_ref[0, p, 0]

    def weighted_values(s, p, pr):
        past = lax.dot_general(pr[:, :BAND_PAST], hvt_ref[s, _lane_cols(p), :].astype(BF16), _NT,
                               preferred_element_type=F32)
        own = jnp.dot(pr[:, BAND_PAST:], v_ref[s, :, _lane_cols(p)], preferred_element_type=F32)
        return past + own

    units = [(s, p) for s in range(S) for p in range(H // 2)]
    _attend_units(units, scores, weighted_values, o_ref, TQ, low)


def _attention(q, kb, vb, hkt, hvt, bias, *, S, TQ):
    nseq, T, AD = q.shape
    H = AD // HEAD_DIM
    TK = BAND_PAST + TQ
    n_var, HP, NB, TQ2, BK = bias.shape
    assert HP * 2 == H and TQ2 == 2 * TQ and NB * BK == TK
    tile = pl.BlockSpec((S, TQ, AD), lambda i, t: (i, t, 0))
    bias_spec = pl.BlockSpec((1, HP, NB, TQ2, BK),
                             lambda i, t: (jnp.minimum(t, n_var - 1), 0, 0, 0, 0))
    out_shape = jax.ShapeDtypeStruct((nseq, T, AD), BF16)
    if hkt is None:
        assert BK == TQ and TQ % LANES == 0
        return pl.pallas_call(
            functools.partial(_attn_ring_kernel, S=S, TQ=TQ, H=H),
            grid=(nseq // S, T // TQ),
            in_specs=[tile, tile, tile, bias_spec],
            out_specs=tile,
            out_shape=out_shape,
            scratch_shapes=[pltpu.VMEM((S, TK, AD), BF16), pltpu.VMEM((S, TK, AD), BF16)],
            compiler_params=_cparams(2),
            name="attn_ring",
        )(q, kb, vb, bias)
    assert T == TQ and NB == 1 and n_var == 1
    hist = pl.BlockSpec((S, AD, BAND_PAST), lambda i, t: (i, 0, 0))
    return pl.pallas_call(
        functools.partial(_attn_cached_kernel, S=S, TQ=TQ, H=H),
        grid=(nseq // S, 1),
        in_specs=[tile, tile, tile, hist, hist, bias_spec],
        out_specs=tile,
        out_shape=out_shape,
        compiler_params=_cparams(2),
        name="attn_cached",
    )(q, kb, vb, hkt, hvt, bias)


def _outffn_kernel(x_ref, mod_ref, u_ref, un_ref, hist_ref, wdw_ref, bdw_ref, lg_ref, lb_ref,
                   ao_ref, gc_ref, ga_ref, g2_ref,
                   wc_ref, wa_ref, wo_ref, wfi_ref, wfo_ref, y_ref, ext, ybuf, cybuf, act,
                   *, S, R, D, F, FC, lookahead):
    rows = S * R
    NS = u_ref.shape[2] // LANES
    t = pl.program_id(1)
    pieces = _conv_pieces(ext, ybuf, wdw_ref, S=S, R=R, NS=NS)
    finish = functools.partial(_conv_finish, ybuf, bdw_ref, lg_ref, lb_ref, S=S, NS=NS)

    def stage(tile_ref):
        for s in range(S):
            for c in range(NS):
                ext[s * NS + c, HIST_ROWS:HIST_ROWS + R, :] = tile_ref[s, :, c * LANES:(c + 1) * LANES]

    @pl.when(t == 0)
    def _():
        for s in range(S):
            for c in range(NS):
                ext[s * NS + c, CONV_LEAD:HIST_ROWS, :] = hist_ref[s, :, c * LANES:(c + 1) * LANES]
        stage(u_ref)
        for piece in pieces:
            piece()
        cybuf[0] = finish()

    cur = t % 2 if lookahead else 0
    cy = cybuf[cur]
    todo = []
    if lookahead:
        for s in range(S):
            for c in range(NS):
                ext[s * NS + c, 0:HIST_ROWS, :] = ext[s * NS + c, R:R + HIST_ROWS, :]
        stage(un_ref)

        def hand_over(_):
            cybuf[1 - cur] = finish()

        todo = list(pieces) + [hand_over]
    n_slots = 3 + F // FC + D // FC
    quota = [len(todo) // n_slots + (i < len(todo) % n_slots) for i in range(n_slots)]

    def then_conv(value):
        for _ in range(quota.pop(0)):
            todo.pop(0)(_order_token(value))
        return value

    attn_out = then_conv(jnp.dot(ao_ref[...].reshape(rows, -1), wa_ref[...],
                                 preferred_element_type=F32))
    conv_out = then_conv(jnp.dot(cy, wc_ref[...], preferred_element_type=F32))
    merged = (gc_ref[...].reshape(rows, D).astype(F32) * conv_out
              + ga_ref[...].reshape(rows, D).astype(F32) * attn_out)
    upd = then_conv(jnp.dot(merged.astype(BF16), wo_ref[...], preferred_element_type=F32))
    gt1 = mod_ref[:, :, 2 * D:3 * D]
    x1 = x_ref[...] + gt1 * upd.reshape(S, R, D)

    ms = jnp.mean(x1 * x1, axis=-1, keepdims=True)
    xn = x1 * lax.rsqrt(ms + NORM_EPS) * g2_ref[...]
    sh2 = mod_ref[:, :, 3 * D:4 * D]
    sc2 = mod_ref[:, :, 4 * D:5 * D]
    h2 = (xn * (1.0 + sc2) + sh2).reshape(rows, D).astype(BF16)

    for c in range(F // FC):
        gate = then_conv(jnp.dot(h2, wfi_ref[:, c * FC:(c + 1) * FC], preferred_element_type=F32))
        up = jnp.dot(h2, wfi_ref[:, F + c * FC:F + (c + 1) * FC], preferred_element_type=F32)
        act[:, c * FC:(c + 1) * FC] = (_silu(gate) * up).astype(BF16)

    gt2 = mod_ref[:, :, 5 * D:6 * D]
    for c in range(D // FC):
        cols = slice(c * FC, (c + 1) * FC)
        ffn = then_conv(jnp.dot(act[...], wfo_ref[:, cols], preferred_element_type=F32))
        y_ref[:, :, cols] = x1[:, :, cols] + gt2[:, :, cols] * ffn.reshape(S, R, FC)
    assert not todo


def _out_ffn(x, mod, u, hist, w_dw, b_dw, ln_g, ln_b, ao, gc, ga, g2, wc, wa, wo, wfi, wfo, *, S, R):
    nseq, T, D = x.shape
    F = wfo.shape[0]
    FC = 256
    CD = u.shape[2]
    NS = CD // LANES
    n_t = T // R
    assert R % CONV_BLOCK_ROWS == 0 and R >= HIST_ROWS
    w_slabs = jnp.pad(w_dw, ((0, -CONV_K % SUBLANES), (0, 0))).reshape(-1, NS, LANES).swapaxes(0, 1)
    tile = lambda c: pl.BlockSpec((S, R, c), lambda i, t: (i, t, 0))
    return pl.pallas_call(
        functools.partial(_outffn_kernel, S=S, R=R, D=D, F=F, FC=FC, lookahead=n_t > 1),
        grid=(nseq // S, n_t),
        in_specs=[tile(D),
                  pl.BlockSpec((S, 1, mod.shape[2]), lambda i, t: (i, 0, 0)),
                  pl.BlockSpec((S, R, CD), lambda i, t: (i, 0, 0)),
                  pl.BlockSpec((S, R, CD), lambda i, t: (i, jnp.minimum(t + 1, n_t - 1), 0)),
                  pl.BlockSpec((S, CONV_K - 1, CD), lambda i, t: (i, 0, 0)),
                  _resident(w_slabs.shape), _resident(b_dw.shape),
                  _resident(ln_g.shape), _resident(ln_b.shape),
                  tile(ao.shape[2]), tile(D), tile(D),
                  _resident(g2.shape), _resident(wc.shape), _resident(wa.shape),
                  _resident(wo.shape), _resident(wfi.shape), _resident(wfo.shape)],
        out_specs=tile(D),
        out_shape=jax.ShapeDtypeStruct((nseq, T, D), F32),
        scratch_shapes=[pltpu.VMEM((S * NS, R + HIST_ROWS, LANES), F32),
                        pltpu.VMEM((S * NS, R, LANES), F32),
                        pltpu.VMEM((2, S * R, CD), BF16),
                        pltpu.VMEM((S * R, F), BF16)],
        compiler_params=_cparams(2),
        name="out_ffn",
    )(x, mod, u, u, hist, w_slabs, b_dw, ln_g, ln_b, ao, gc, ga, g2, wc, wa, wo, wfi, wfo)


def _tiling(nseq, T, rows):
    R = min(T, rows)
    S = max(1, min(nseq, rows // R))
    assert T % R == 0 and nseq % S == 0
    return S, R


def _layer(x, mod, conv_hist, hk, hv, bias, p, *, TQ):
    nseq, T, D = x.shape
    Si, Ri = _tiling(nseq, T, IN_PROJ_ROWS)
    u, q, kf, vf, kb, vb, gc, ga = _in_proj(
        x, mod, p["g1"], p["w_in"], p["qg"], p["kg"], p["gm"], S=Si, R=Ri)
    S, R = _tiling(nseq, T, OUT_FFN_ROWS)
    Sa, _ = _tiling(nseq, T, max(TQ, ATTN_ROWS))
    ao = _attention(q, kb, vb, hk, hv, bias, S=Sa, TQ=TQ)
    y = _out_ffn(x, mod, u, conv_hist, p["w_dw"], p["b_dw"], p["ln_g"], p["ln_b"], ao, gc, ga,
                 p["g2"], p["w_conv_out"], p["w_attn_out"], p["w_o"], p["w_ffn_in"], p["w_ffn_out"],
                 S=S, R=R)
    return y, u, kf, vf


def kernel(x_prompt, x_sample, c_prompt, c_sample, cache_conv, cache_k, cache_v, norm1_g, norm2_g, w_ada, b_ada, w_in, w_dw, b_dw, conv_ln_g, conv_ln_b, w_conv_out, q_norm_g, k_norm_g, rel_bias, w_attn_out, w_o, w_ffn_in, w_ffn_out):
    depth = norm1_g.shape[0]
    B, T, D = x_prompt.shape
    BS, TS, _ = x_sample.shape
    H, Dh = cache_k.shape[3], cache_k.shape[4]
    AD = H * Dh
    CD = w_dw.shape[2]
    cache_len = cache_k.shape[2]
    assert cache_len == BAND_PAST and TS == CHUNK and T % CHUNK == 0 and Dh == HEAD_DIM
    state_len = min(BAND_PAST, T)
    TQP = 256
    n_c = B + BS
    n_pad = -n_c % 8

    hid = jnp.arange(MXU_DIM) // Dh
    gm = jnp.where(hid[:, None] == hid[None, :], 1.0 / Dh, 0.0).astype(BF16)

    xp, xs = x_prompt, x_sample
    outs = [[] for _ in range(6)]
    for l in range(depth):
        p = {
            "g1": norm1_g[l].reshape(1, D), "g2": norm2_g[l].reshape(1, D),
            "w_in": w_in[l].astype(BF16),
            "qg": jnp.tile(q_norm_g[l], H).reshape(1, AD), "kg": jnp.tile(k_norm_g[l], H).reshape(1, AD),
            "gm": gm,
            "w_dw": w_dw[l], "b_dw": b_dw[l].reshape(1, CD),
            "ln_g": conv_ln_g[l].reshape(1, CD), "ln_b": conv_ln_b[l].reshape(1, CD),
            "w_conv_out": w_conv_out[l].astype(BF16), "w_attn_out": w_attn_out[l].astype(BF16),
            "w_o": w_o[l].astype(BF16), "w_ffn_in": w_ffn_in[l].astype(BF16),
            "w_ffn_out": w_ffn_out[l].astype(BF16),
        }
        c_all = jnp.pad(jnp.concatenate([c_prompt, c_sample], axis=0), ((0, n_pad), (0, 0)))
        mod = _ada(c_all, w_ada[l], b_ada[l])
        mod_p = mod[:B].reshape(B, 1, -1)
        mod_s = mod[B:n_c].reshape(BS, 1, -1)

        bias_p = _bias_tiles(rel_bias[l], TQ=TQP, n_var=BAND_PAST // TQP + 1, first_start=0,
                             NB=BAND_PAST // TQP + 1)
        bias_s = _bias_tiles(rel_bias[l], TQ=TS, n_var=1, first_start=BAND_PAST, NB=1)

        zeros_c = jnp.zeros((B, CONV_K - 1, CD), F32)
        xp, u_p, kf_p, vf_p = _layer(xp, mod_p, zeros_c, None, None, bias_p, p, TQ=TQP)
        hkt = jnp.transpose(cache_k[l], (0, 2, 3, 1)).reshape(BS, AD, cache_len)
        hvt = jnp.transpose(cache_v[l], (0, 2, 3, 1)).reshape(BS, AD, cache_len)
        xs, u_s, kf_s, vf_s = _layer(xs, mod_s, cache_conv[l], hkt, hvt, bias_s, p, TQ=TS)

        outs[0].append(u_p[:, T - (CONV_K - 1):])
        assert state_len <= kf_p.shape[1]
        outs[1].append(kf_p[:, kf_p.shape[1] - state_len:].reshape(B, state_len, H, Dh))
        outs[2].append(vf_p[:, vf_p.shape[1] - state_len:].reshape(B, state_len, H, Dh))
        outs[3].append(u_s[:, TS - (CONV_K - 1):])
        outs[4].append(kf_s.reshape(BS, TS, H, Dh))
        outs[5].append(vf_s.reshape(BS, TS, H, Dh))

    return (xp, xs) + tuple(jnp.stack(o) for o in outs)
```

```python
import functools
import math

import jax
import jax.numpy as jnp
from jax import lax
from jax.experimental import pallas as pl
from jax.experimental.pallas import tpu as pltpu

F32 = jnp.float32
BF16 = jnp.bfloat16

CHUNK = 64
N_PREV_CHUNKS = 8
BAND_PAST = N_PREV_CHUNKS * CHUNK
HEAD_DIM = 64
CONV_K = 31
MAX_REL = 128
NORM_EPS = 1e-6
NEG_INF = -1e30
LOG2_E = math.log2(math.e)
SCORE_SCALE = LOG2_E / math.sqrt(HEAD_DIM)
LANES = 128
SUBLANES = 8
MXU_DIM = 256
CONV_ROW_STRIDE = 4
HIST_ROWS = 32
ROLL_W = 1024
VMEM_LIMIT = 56 * 1024 * 1024
IN_PROJ_ROWS = 1024
OUT_FFN_ROWS = 512
ATTN_ROWS = 256


def _cparams(n_axes):
    return pltpu.CompilerParams(
        dimension_semantics=("arbitrary",) * n_axes, vmem_limit_bytes=VMEM_LIMIT)


def _resident(shape):
    nd = len(shape)
    return pl.BlockSpec(shape, lambda *_: (0,) * nd, pipeline_mode=pl.Buffered(1))


def _sigmoid(x):
    return 0.5 * jnp.tanh(0.5 * x) + 0.5


def _silu(x):
    return x * _sigmoid(x)


def _ada_kernel(c_ref, w_ref, b_ref, o_ref):
    a = _silu(c_ref[...]).astype(BF16)
    o_ref[...] = jnp.dot(a, w_ref[...].astype(BF16), preferred_element_type=F32) + b_ref[...]


def _ada(c_all, w_ada, b_ada):
    n, d = c_all.shape
    n_out = w_ada.shape[1]
    tn = 1024
    return pl.pallas_call(
        _ada_kernel,
        grid=(n_out // tn,),
        in_specs=[pl.BlockSpec((n, d), lambda j: (0, 0)),
                  pl.BlockSpec((d, tn), lambda j: (0, j)),
                  pl.BlockSpec((1, tn), lambda j: (0, j))],
        out_specs=pl.BlockSpec((n, tn), lambda j: (0, j)),
        out_shape=jax.ShapeDtypeStruct((n, n_out), F32),
        compiler_params=_cparams(1),
        name="ada",
    )(c_all, w_ada, b_ada.reshape(1, n_out))


CONV_LEAD = HIST_ROWS - (CONV_K - 1)
CONV_BLOCK_ROWS = SUBLANES * CONV_ROW_STRIDE
CONV_GROUP = 4


def _order_token(x):
    bits = pltpu.bitcast(x[:SUBLANES, :LANES].astype(F32), jnp.uint32)
    return lax.shift_right_logical(lax.shift_right_logical(bits, jnp.uint32(16)), jnp.uint32(16))


def _ordered_after(x, token):
    return pltpu.bitcast(pltpu.bitcast(x, jnp.uint32) | token, x.dtype)


def _conv_pieces(ext, ybuf, wdw_ref, *, S, R, NS):
    ST = CONV_ROW_STRIDE
    RB = CONV_BLOCK_ROWS
    G = min(CONV_GROUP, R // RB)

    def piece(slab, c, g0, after=None):
        accs = [None] * (G * ST)
        for j in range(CONV_K):
            tap = jnp.broadcast_to(wdw_ref[c, j:j + 1, :], (SUBLANES, LANES))
            if after is not None:
                tap = _ordered_after(tap, after)
            for b in range(G):
                for q in range(ST):
                    start = g0 + b * RB + CONV_LEAD + q + j
                    term = ext[slab, pl.ds(start, SUBLANES, stride=ST), :] * tap
                    a = b * ST + q
                    accs[a] = term if accs[a] is None else accs[a] + term
        for b in range(G):
            for q in range(ST):
                ybuf[slab, pl.ds(g0 + b * RB + q, SUBLANES, stride=ST), :] = accs[b * ST + q]

    return [functools.partial(piece, s * NS + c, c, g0)
            for s in range(S) for c in range(NS) for g0 in range(0, R, RB * G)]


def _conv_finish(ybuf, bdw_ref, lg_ref, lb_ref, *, S, NS):
    y = jnp.concatenate(
        [jnp.concatenate([ybuf[s * NS + c] for c in range(NS)], axis=-1) for s in range(S)], axis=0)
    y = y + bdw_ref[...]
    mu = jnp.mean(y, axis=-1, keepdims=True)
    yc = y - mu
    var = jnp.mean(yc * yc, axis=-1, keepdims=True)
    yn = yc * lax.rsqrt(var + NORM_EPS) * lg_ref[...] + lb_ref[...]
    return _silu(yn).astype(BF16)


def _inproj_kernel(x_ref, mod_ref, g1_ref, w_ref, qg_ref, kg_ref, gm_ref,
                   u_ref, q_ref, kf_ref, vf_ref, kb_ref, vb_ref, gc_ref, ga_ref,
                   *, S, R, D, CD, AD):
    rows = S * R

    x = x_ref[...]
    ms = jnp.mean(x * x, axis=-1, keepdims=True)
    xn = x * lax.rsqrt(ms + NORM_EPS) * g1_ref[...]
    sh = mod_ref[:, :, 0:D]
    sc = mod_ref[:, :, D:2 * D]
    h = (xn * (1.0 + sc) + sh).reshape(rows, D).astype(BF16)

    def sec(lo, width):
        return jnp.dot(h, w_ref[:, lo:lo + width], preferred_element_type=F32)

    def head_rms(z, g_ref):
        zz = (z * z).astype(BF16)
        gw = gm_ref.shape[0]
        msq = jnp.concatenate(
            [jnp.dot(zz[:, c:c + gw], gm_ref[...], preferred_element_type=F32)
             for c in range(0, AD, gw)], axis=-1)
        return z * lax.rsqrt(msq + NORM_EPS) * g_ref[...]

    u = sec(0, CD) * _sigmoid(sec(CD, CD))
    u_ref[...] = u.reshape(S, R, CD)

    o = 2 * CD
    qn = head_rms(sec(o, AD), qg_ref) * SCORE_SCALE
    q_ref[...] = qn.astype(BF16).reshape(S, R, AD)
    kn = head_rms(sec(o + AD, AD), kg_ref)
    kf_ref[...] = kn.reshape(S, R, AD)
    kb_ref[...] = kn.astype(BF16).reshape(S, R, AD)
    v = sec(o + 2 * AD, AD)
    vf_ref[...] = v.reshape(S, R, AD)
    vb_ref[...] = v.astype(BF16).reshape(S, R, AD)
    o = o + 3 * AD
    gc_ref[...] = _sigmoid(sec(o, D)).astype(BF16).reshape(S, R, D)
    ga_ref[...] = _sigmoid(sec(o + D, D)).astype(BF16).reshape(S, R, D)


def _in_proj(x, mod, g1, w_in, qg, kg, gm, *, S, R):
    nseq, T, D = x.shape
    AD = qg.shape[1]
    CD = (w_in.shape[1] - 3 * AD - 2 * D) // 2
    grid = (nseq // S, T // R)
    tile = lambda c: pl.BlockSpec((S, R, c), lambda i, t: (i, t, 0))
    last = lambda c: pl.BlockSpec((S, R, c), lambda i, t: (i, 0, 0))
    outs = [(CD, F32, tile, T), (AD, BF16, tile, T), (AD, F32, last, R), (AD, F32, last, R),
            (AD, BF16, tile, T), (AD, BF16, tile, T), (D, BF16, tile, T), (D, BF16, tile, T)]
    return pl.pallas_call(
        functools.partial(_inproj_kernel, S=S, R=R, D=D, CD=CD, AD=AD),
        grid=grid,
        in_specs=[tile(D),
                  pl.BlockSpec((S, 1, mod.shape[2]), lambda i, t: (i, 0, 0)),
                  _resident(g1.shape), _resident(w_in.shape), _resident(qg.shape),
                  _resident(kg.shape), _resident(gm.shape)],
        out_specs=[spec(c) for c, _, spec, _ in outs],
        out_shape=[jax.ShapeDtypeStruct((nseq, n, c), dt) for c, dt, _, n in outs],
        compiler_params=_cparams(2),
        name="in_proj",
    )(x, mod, g1, w_in, qg, kg, gm)


def _bias_kernel(e_ref, o_ref, *, TQ, TK, NB, first_start):
    start = first_start + pl.program_id(0) * TQ
    qi = lax.broadcasted_iota(jnp.int32, (TQ, TK), 0)
    ki = lax.broadcasted_iota(jnp.int32, (TQ, TK), 1)
    qc = qi // CHUNK
    kc = ki // CHUNK
    valid = (kc >= qc) & (kc <= qc + N_PREV_CHUNKS) & (ki + start >= BAND_PAST)
    BK = TK // NB
    for h in range(e_ref.shape[0]):
        row = jnp.broadcast_to(e_ref[h], (TQ, ROLL_W))
        b = pltpu.roll(row, 0, 1, stride=1, stride_axis=0)[:, :TK]
        tile = jnp.where(valid, b * LOG2_E, NEG_INF)
        for j in range(NB):
            o_ref[0, h // 2, j, (h % 2) * TQ:(h % 2 + 1) * TQ, :] = tile[:, j * BK:(j + 1) * BK]


def _bias_tiles(rel_bias, *, TQ, n_var, first_start, NB):
    H = rel_bias.shape[0]
    TK = BAND_PAST + TQ
    assert TQ + TK <= ROLL_W
    far = rel_bias[:, 2 * MAX_REL:]
    near = rel_bias[:, :1]
    by_m = jnp.concatenate([
        jnp.broadcast_to(far, (H, BAND_PAST - MAX_REL)),
        rel_bias[:, ::-1],
        jnp.broadcast_to(near, (H, ROLL_W - (BAND_PAST + MAX_REL) - 1)),
    ], axis=1)
    e = jnp.concatenate([by_m[:, :TK], jnp.broadcast_to(far, (H, ROLL_W - TK))], axis=1)
    e = e.reshape(H, 1, ROLL_W)
    return pl.pallas_call(
        functools.partial(_bias_kernel, TQ=TQ, TK=TK, NB=NB, first_start=first_start),
        grid=(n_var,),
        in_specs=[pl.BlockSpec((H, 1, ROLL_W), lambda j: (0, 0, 0))],
        out_specs=pl.BlockSpec((1, H // 2, NB, 2 * TQ, TK // NB), lambda j: (j, 0, 0, 0, 0)),
        out_shape=jax.ShapeDtypeStruct((n_var, H // 2, NB, 2 * TQ, TK // NB), F32),
        compiler_params=_cparams(1),
        name="bias_tiles",
    )(e)


_NT = (((1,), (1,)), ((), ()))
SCORE_LOOKAHEAD = 2


def _lane_cols(p):
    return slice(p * LANES, (p + 1) * LANES)


def _first_head_lanes():
    return lax.broadcasted_iota(jnp.int32, (1, LANES), 1) < HEAD_DIM


def _stack_pair_queries(q2, low):
    zero = jnp.zeros_like(q2)
    return jnp.concatenate([jnp.where(low, q2, zero), jnp.where(low, zero, q2)], axis=0)


def _attend_units(units, scores, weighted_values, o_ref, TQ, low):
    ahead = [scores(*u) for u in units[:SCORE_LOOKAHEAD]]
    for n, (s, p) in enumerate(units):
        sc = ahead.pop(0)
        if n + SCORE_LOOKAHEAD < len(units):
            ahead.append(scores(*units[n + SCORE_LOOKAHEAD]))
        mx = jnp.max(sc, axis=-1, keepdims=True)
        pr = jnp.exp2(sc - mx)
        den = jnp.sum(pr, axis=-1, keepdims=True)
        oo = weighted_values(s, p, pr.astype(BF16)) / den
        o_ref[s, :, _lane_cols(p)] = jnp.where(low, oo[:TQ], oo[TQ:]).astype(BF16)


def _attn_ring_kernel(q_ref, k_ref, v_ref, bias_ref, o_ref, kbuf, vbuf, *, S, TQ, H):
    t = pl.program_id(1)
    NS = kbuf.shape[1] // TQ
    cur = (t + NS - 1) % NS

    @pl.when(t == 0)
    def _():
        kbuf[...] = jnp.zeros_like(kbuf)
        vbuf[...] = jnp.zeros_like(vbuf)

    row0 = pl.multiple_of(cur * TQ, TQ)
    for s in range(S):
        kbuf[s, pl.ds(row0, TQ), :] = k_ref[s]
        vbuf[s, pl.ds(row0, TQ), :] = v_ref[s]

    low = _first_head_lanes()
    logical = [(j + NS - 1 - cur) % NS for j in range(NS)]

    def scores(s, p):
        qq = _stack_pair_queries(q_ref[s, :, _lane_cols(p)], low)
        sc = lax.dot_general(qq, kbuf[s, :, _lane_cols(p)], _NT, preferred_element_type=F32)
        return sc + jnp.concatenate([bias_ref[0, p, logical[j]] for j in range(NS)], axis=-1)

    def weighted_values(s, p, pr):
        return jnp.dot(pr, vbuf[s, :, _lane_cols(p)], preferred_element_type=F32)

    units = [(s, p) for s in range(S) for p in range(H // 2)]
    _attend_units(units, scores, weighted_values, o_ref, TQ, low)


def _attn_cached_kernel(q_ref, k_ref, v_ref, hkt_ref, hvt_ref, bias_ref, o_ref, *, S, TQ, H):
    low = _first_head_lanes()

    def scores(s, p):
        qq = _stack_pair_queries(q_ref[s, :, _lane_cols(p)], low)
        past = jnp.dot(qq, hkt_ref[s, _lane_cols(p), :].astype(BF16), preferred_element_type=F32)
        own = lax.dot_general(qq, k_ref[s, :, _lane_cols(p)], _NT, preferred_element_type=F32)
        return jnp.concatenate([past, own], axis=-1) + bias_ref[0, p, 0]

    def weighted_values(s, p, pr):
        past = lax.dot_general(pr[:, :BAND_PAST], hvt_ref[s, _lane_cols(p), :].astype(BF16), _NT,
                               preferred_element_type=F32)
        own = jnp.dot(pr[:, BAND_PAST:], v_ref[s, :, _lane_cols(p)], preferred_element_type=F32)
        return past + own

    units = [(s, p) for s in range(S) for p in range(H // 2)]
    _attend_units(units, scores, weighted_values, o_ref, TQ, low)


def _attention(q, kb, vb, hkt, hvt, bias, *, S, TQ):
    nseq, T, AD = q.shape
    H = AD // HEAD_DIM
    TK = BAND_PAST + TQ
    n_var, HP, NB, TQ2, BK = bias.shape
    assert HP * 2 == H and TQ2 == 2 * TQ and NB * BK == TK
    tile = pl.BlockSpec((S, TQ, AD), lambda i, t: (i, t, 0))
    bias_spec = pl.BlockSpec((1, HP, NB, TQ2, BK),
                             lambda i, t: (jnp.minimum(t, n_var - 1), 0, 0, 0, 0))
    out_shape = jax.ShapeDtypeStruct((nseq, T, AD), BF16)
    if hkt is None:
        assert BK == TQ and TQ % LANES == 0
        return pl.pallas_call(
            functools.partial(_attn_ring_kernel, S=S, TQ=TQ, H=H),
            grid=(nseq // S, T // TQ),
            in_specs=[tile, tile, tile, bias_spec],
            out_specs=tile,
            out_shape=out_shape,
            scratch_shapes=[pltpu.VMEM((S, TK, AD), BF16), pltpu.VMEM((S, TK, AD), BF16)],
            compiler_params=_cparams(2),
            name="attn_ring",
        )(q, kb, vb, bias)
    assert T == TQ and NB == 1 and n_var == 1
    hist = pl.BlockSpec((S, AD, BAND_PAST), lambda i, t: (i, 0, 0))
    return pl.pallas_call(
        functools.partial(_attn_cached_kernel, S=S, TQ=TQ, H=H),
        grid=(nseq // S, 1),
        in_specs=[tile, tile, tile, hist, hist, bias_spec],
        out_specs=tile,
        out_shape=out_shape,
        compiler_params=_cparams(2),
        name="attn_cached",
    )(q, kb, vb, hkt, hvt, bias)


def _outffn_kernel(x_ref, mod_ref, u_ref, un_ref, hist_ref, wdw_ref, bdw_ref, lg_ref, lb_ref,
                   ao_ref, gc_ref, ga_ref, g2_ref,
                   wc_ref, wa_ref, wo_ref, wfi_ref, wfo_ref, y_ref, ext, ybuf, cybuf, act,
                   *, S, R, D, F, FC, lookahead):
    rows = S * R
    NS = u_ref.shape[2] // LANES
    t = pl.program_id(1)
    pieces = _conv_pieces(ext, ybuf, wdw_ref, S=S, R=R, NS=NS)
    finish = functools.partial(_conv_finish, ybuf, bdw_ref, lg_ref, lb_ref, S=S, NS=NS)

    def stage(tile_ref):
        for s in range(S):
            for c in range(NS):
                ext[s * NS + c, HIST_ROWS:HIST_ROWS + R, :] = tile_ref[s, :, c * LANES:(c + 1) * LANES]

    @pl.when(t == 0)
    def _():
        for s in range(S):
            for c in range(NS):
                ext[s * NS + c, CONV_LEAD:HIST_ROWS, :] = hist_ref[s, :, c * LANES:(c + 1) * LANES]
        stage(u_ref)
        for piece in pieces:
            piece()
        cybuf[0] = finish()

    cur = t % 2 if lookahead else 0
    cy = cybuf[cur]
    todo = []
    if lookahead:
        for s in range(S):
            for c in range(NS):
                ext[s * NS + c, 0:HIST_ROWS, :] = ext[s * NS + c, R:R + HIST_ROWS, :]
        stage(un_ref)

        def hand_over(_):
            cybuf[1 - cur] = finish()

        todo = list(pieces) + [hand_over]
    n_slots = 3 + F // FC + D // FC
    quota = [len(todo) // n_slots + (i < len(todo) % n_slots) for i in range(n_slots)]

    def then_conv(value):
        for _ in range(quota.pop(0)):
            todo.pop(0)(_order_token(value))
        return value

    attn_out = then_conv(jnp.dot(ao_ref[...].reshape(rows, -1), wa_ref[...],
                                 preferred_element_type=F32))
    conv_out = then_conv(jnp.dot(cy, wc_ref[...], preferred_element_type=F32))
    merged = (gc_ref[...].reshape(rows, D).astype(F32) * conv_out
              + ga_ref[...].reshape(rows, D).astype(F32) * attn_out)
    upd = then_conv(jnp.dot(merged.astype(BF16), wo_ref[...], preferred_element_type=F32))
    gt1 = mod_ref[:, :, 2 * D:3 * D]
    x1 = x_ref[...] + gt1 * upd.reshape(S, R, D)

    ms = jnp.mean(x1 * x1, axis=-1, keepdims=True)
    xn = x1 * lax.rsqrt(ms + NORM_EPS) * g2_ref[...]
    sh2 = mod_ref[:, :, 3 * D:4 * D]
    sc2 = mod_ref[:, :, 4 * D:5 * D]
    h2 = (xn * (1.0 + sc2) + sh2).reshape(rows, D).astype(BF16)

    for c in range(F // FC):
        gate = then_conv(jnp.dot(h2, wfi_ref[:, c * FC:(c + 1) * FC], preferred_element_type=F32))
        up = jnp.dot(h2, wfi_ref[:, F + c * FC:F + (c + 1) * FC], preferred_element_type=F32)
        act[:, c * FC:(c + 1) * FC] = (_silu(gate) * up).astype(BF16)

    gt2 = mod_ref[:, :, 5 * D:6 * D]
    for c in range(D // FC):
        cols = slice(c * FC, (c + 1) * FC)
        ffn = then_conv(jnp.dot(act[...], wfo_ref[:, cols], preferred_element_type=F32))
        y_ref[:, :, cols] = x1[:, :, cols] + gt2[:, :, cols] * ffn.reshape(S, R, FC)
    assert not todo


def _out_ffn(x, mod, u, hist, w_dw, b_dw, ln_g, ln_b, ao, gc, ga, g2, wc, wa, wo, wfi, wfo, *, S, R):
    nseq, T, D = x.shape
    F = wfo.shape[0]
    FC = 256
    CD = u.shape[2]
    NS = CD // LANES
    n_t = T // R
    assert R % CONV_BLOCK_ROWS == 0 and R >= HIST_ROWS
    w_slabs = jnp.pad(w_dw, ((0, -CONV_K % SUBLANES), (0, 0))).reshape(-1, NS, LANES).swapaxes(0, 1)
    tile = lambda c: pl.BlockSpec((S, R, c), lambda i, t: (i, t, 0))
    return pl.pallas_call(
        functools.partial(_outffn_kernel, S=S, R=R, D=D, F=F, FC=FC, lookahead=n_t > 1),
        grid=(nseq // S, n_t),
        in_specs=[tile(D),
                  pl.BlockSpec((S, 1, mod.shape[2]), lambda i, t: (i, 0, 0)),
                  pl.BlockSpec((S, R, CD), lambda i, t: (i, 0, 0)),
                  pl.BlockSpec((S, R, CD), lambda i, t: (i, jnp.minimum(t + 1, n_t - 1), 0)),
                  pl.BlockSpec((S, CONV_K - 1, CD), lambda i, t: (i, 0, 0)),
                  _resident(w_slabs.shape), _resident(b_dw.shape),
                  _resident(ln_g.shape), _resident(ln_b.shape),
                  tile(ao.shape[2]), tile(D), tile(D),
                  _resident(g2.shape), _resident(wc.shape), _resident(wa.shape),
                  _resident(wo.shape), _resident(wfi.shape), _resident(wfo.shape)],
        out_specs=tile(D),
        out_shape=jax.ShapeDtypeStruct((nseq, T, D), F32),
        scratch_shapes=[pltpu.VMEM((S * NS, R + HIST_ROWS, LANES), F32),
                        pltpu.VMEM((S * NS, R, LANES), F32),
                        pltpu.VMEM((2, S * R, CD), BF16),
                        pltpu.VMEM((S * R, F), BF16)],
        compiler_params=_cparams(2),
        name="out_ffn",
    )(x, mod, u, u, hist, w_slabs, b_dw, ln_g, ln_b, ao, gc, ga, g2, wc, wa, wo, wfi, wfo)


def _tiling(nseq, T, rows):
    R = min(T, rows)
    S = max(1, min(nseq, rows // R))
    assert T % R == 0 and nseq % S == 0
    return S, R


def _layer(x, mod, conv_hist, hk, hv, bias, p, *, TQ):
    nseq, T, D = x.shape
    Si, Ri = _tiling(nseq, T, IN_PROJ_ROWS)
    u, q, kf, vf, kb, vb, gc, ga = _in_proj(
        x, mod, p["g1"], p["w_in"], p["qg"], p["kg"], p["gm"], S=Si, R=Ri)
    S, R = _tiling(nseq, T, OUT_FFN_ROWS)
    Sa, _ = _tiling(nseq, T, max(TQ, ATTN_ROWS))
    ao = _attention(q, kb, vb, hk, hv, bias, S=Sa, TQ=TQ)
    y = _out_ffn(x, mod, u, conv_hist, p["w_dw"], p["b_dw"], p["ln_g"], p["ln_b"], ao, gc, ga,
                 p["g2"], p["w_conv_out"], p["w_attn_out"], p["w_o"], p["w_ffn_in"], p["w_ffn_out"],
                 S=S, R=R)
    return y, u, kf, vf


def kernel(x_prompt, x_sample, c_prompt, c_sample, cache_conv, cache_k, cache_v, norm1_g, norm2_g, w_ada, b_ada, w_in, w_dw, b_dw, conv_ln_g, conv_ln_b, w_conv_out, q_norm_g, k_norm_g, rel_bias, w_attn_out, w_o, w_ffn_in, w_ffn_out):
    depth = norm1_g.shape[0]
    B, T, D = x_prompt.shape
    BS, TS, _ = x_sample.shape
    H, Dh = cache_k.shape[3], cache_k.shape[4]
    AD = H * Dh
    CD = w_dw.shape[2]
    cache_len = cache_k.shape[2]
    assert cache_len == BAND_PAST and TS == CHUNK and T % CHUNK == 0 and Dh == HEAD_DIM
    state_len = min(BAND_PAST, T)
    TQP = 256
    n_c = B + BS
    n_pad = -n_c % 8

    hid = jnp.arange(MXU_DIM) // Dh
    gm = jnp.where(hid[:, None] == hid[None, :], 1.0 / Dh, 0.0).astype(BF16)

    xp, xs = x_prompt, x_sample
    outs = [[] for _ in range(6)]
    for l in range(depth):
        p = {
            "g1": norm1_g[l].reshape(1, D), "g2": norm2_g[l].reshape(1, D),
            "w_in": w_in[l].astype(BF16),
            "qg": jnp.tile(q_norm_g[l], H).reshape(1, AD), "kg": jnp.tile(k_norm_g[l], H).reshape(1, AD),
            "gm": gm,
            "w_dw": w_dw[l], "b_dw": b_dw[l].reshape(1, CD),
            "ln_g": conv_ln_g[l].reshape(1, CD), "ln_b": conv_ln_b[l].reshape(1, CD),
            "w_conv_out": w_conv_out[l].astype(BF16), "w_attn_out": w_attn_out[l].astype(BF16),
            "w_o": w_o[l].astype(BF16), "w_ffn_in": w_ffn_in[l].astype(BF16),
            "w_ffn_out": w_ffn_out[l].astype(BF16),
        }
        c_all = jnp.pad(jnp.concatenate([c_prompt, c_sample], axis=0), ((0, n_pad), (0, 0)))
        mod = _ada(c_all, w_ada[l], b_ada[l])
        mod_p = mod[:B].reshape(B, 1, -1)
        mod_s = mod[B:n_c].reshape(BS, 1, -1)

        bias_p = _bias_tiles(rel_bias[l], TQ=TQP, n_var=BAND_PAST // TQP + 1, first_start=0,
                             NB=BAND_PAST // TQP + 1)
        bias_s = _bias_tiles(rel_bias[l], TQ=TS, n_var=1, first_start=BAND_PAST, NB=1)

        zeros_c = jnp.zeros((B, CONV_K - 1, CD), F32)
        xp, u_p, kf_p, vf_p = _layer(xp, mod_p, zeros_c, None, None, bias_p, p, TQ=TQP)
        hkt = jnp.transpose(cache_k[l], (0, 2, 3, 1)).reshape(BS, AD, cache_len)
        hvt = jnp.transpose(cache_v[l], (0, 2, 3, 1)).reshape(BS, AD, cache_len)
        xs, u_s, kf_s, vf_s = _layer(xs, mod_s, cache_conv[l], hkt, hvt, bias_s, p, TQ=TS)

        outs[0].append(u_p[:, T - (CONV_K - 1):])
        assert state_len <= kf_p.shape[1]
        outs[1].append(kf_p[:, kf_p.shape[1] - state_len:].reshape(B, state_len, H, Dh))
        outs[2].append(vf_p[:, vf_p.shape[1] - state_len:].reshape(B, state_len, H, Dh))
        outs[3].append(u_s[:, TS - (CONV_K - 1):])
        outs[4].append(kf_s.reshape(BS, TS, H, Dh))
        outs[5].append(vf_s.reshape(BS, TS, H, Dh))

    return (xp, xs) + tuple(jnp.stack(o) for o in outs)
```
